```python
import math
import jax, jax.numpy as jnp
from jax import lax
import numpy as np

D_MODEL = 1024
BATCH = 8
SEQ = 2048
DEPTH = 4
DEC_BATCH = 128
DEC_SEQ = 4
PAST_LEN = 2048
PAGE_SIZE = 128

D_CONV = D_MODEL
CONV_W = 3
N_HEADS = 16
N_KV = 4
GROUP = N_HEADS // N_KV
HEAD_DIM = 64
D_ATT = N_HEADS * HEAD_DIM
D_KV = N_KV * HEAD_DIM
CMP_BLOCK = 32
CMP_STRIDE = 16
SEL_BLOCK = 64
N_SEL = 8
WINDOW = 512
Q_BLOCK = 128
FORCED = 1e4
N_BUCKETS = 32
MAX_DISTANCE = 128
EPS = 1e-6
NEG = -1e30
W_SECTIONS = (D_CONV, D_CONV, D_CONV, D_CONV,
              D_ATT, D_KV, D_KV, D_KV, D_KV, D_KV, D_KV,
              3 * N_HEADS, D_ATT,
              D_MODEL, D_MODEL)
D_IN = sum(W_SECTIONS)

kernel_name = 'nsa_shortconv_gated_hybrid_step'


def rmsnorm(x, g):
    xf = x.astype(jnp.float32)
    y = xf * lax.rsqrt(jnp.mean(xf * xf, axis=-1, keepdims=True) + EPS)
    return (y * g.astype(jnp.float32)).astype(x.dtype)


def split_sections(p):
    return jnp.split(p, np.cumsum(W_SECTIONS)[:-1].tolist(), axis=-1)


def rel_bucket(d):
    d = jnp.maximum(d, 0)
    exact = N_BUCKETS // 2
    df = jnp.maximum(d, 1).astype(jnp.float32)
    large = exact + (jnp.log(df / exact) / math.log(MAX_DISTANCE / exact) * (N_BUCKETS - exact)).astype(jnp.int32)
    return jnp.where(d < exact, d, jnp.minimum(large, N_BUCKETS - 1))


def head_bias(rel_bias, d):
    b = rel_bias[rel_bucket(d)].astype(jnp.float32)
    return b.reshape(d.shape + (N_KV, GROUP)).transpose(2, 3, 0, 1)


def masked_softmax(s, mask):
    s = jnp.where(mask, s, NEG)
    p = jnp.where(mask, jnp.exp(s - jnp.max(s, axis=-1, keepdims=True)), 0.0)
    return p / jnp.maximum(jnp.sum(p, axis=-1, keepdims=True), 1e-30)


def compress(k, w, pe):
    B, L = k.shape[:2]
    r = CMP_BLOCK // CMP_STRIDE
    nch = L // CMP_STRIDE
    nc = nch - r + 1
    ch = k[:, :nch * CMP_STRIDE].reshape(B, nch, CMP_STRIDE, N_KV, HEAD_DIM)
    blocks = jnp.concatenate([ch[:, i:i + nc] for i in range(r)], axis=2)
    blocks = blocks + pe[None, None, :, None, :]
    out = jnp.einsum('bcjkd,jde->bcke', blocks, w)
    end = jnp.arange(nc) * CMP_STRIDE + CMP_BLOCK - 1
    return out, end


def to_blocks(k, nb):
    B, L = k.shape[:2]
    k = jnp.pad(k, ((0, 0), (0, nb * SEL_BLOCK - L), (0, 0), (0, 0)))
    return k.reshape(B, nb, SEL_BLOCK, N_KV, HEAD_DIM).transpose(0, 3, 1, 2, 4)


def cmp_to_sel(nc, nb):
    c0 = jnp.arange(nc)[:, None] * CMP_STRIDE
    j0 = jnp.arange(nb)[None, :] * SEL_BLOCK
    ov = jnp.minimum(c0 + CMP_BLOCK, j0 + SEL_BLOCK) - jnp.maximum(c0, j0)
    return jnp.maximum(ov, 0).astype(jnp.float32) / CMP_BLOCK


def nsa_context(kc, vc, ks, vs, w_cmp, pe_cmp):
    f = lambda t: t.astype(jnp.float32)
    k_cmp, cmp_end = compress(f(kc), f(w_cmp[0]), f(pe_cmp[0]))
    v_cmp, _ = compress(f(vc), f(w_cmp[1]), f(pe_cmp[1]))
    nb = -(-ks.shape[1] // SEL_BLOCK)
    return k_cmp, v_cmp, cmp_end, to_blocks(f(ks), nb), to_blocks(f(vs), nb)


def nsa_block(q, qpos, k_cmp, v_cmp, cmp_end, k_sel, v_sel, k_win, v_win, win_pos, gates, rel_bias):
    B, Tq = q.shape[:2]
    nb = k_sel.shape[2]
    k_win = k_win.astype(jnp.float32)
    v_win = v_win.astype(jnp.float32)
    dc = qpos[:, None] - cmp_end[None, :]
    s = jnp.einsum('bqkgd,bckd->bkgqc', q, k_cmp) + head_bias(rel_bias, dc)
    p_cmp = masked_softmax(s, dc >= 0)
    o_cmp = jnp.einsum('bkgqc,bckd->bqkgd', p_cmp, v_cmp)
    imp = jnp.einsum('bkgqc,cj->bkqj', p_cmp, cmp_to_sel(cmp_end.shape[0], nb))
    blk = jnp.arange(nb)[None, :]
    cur = (qpos // SEL_BLOCK)[:, None]
    forced = (blk == 0) | (blk == cur) | (blk == cur - 1)
    valid = blk * SEL_BLOCK <= qpos[:, None]
    score = jnp.where(valid, jnp.where(forced, FORCED, imp), -jnp.inf)
    _, idx = lax.top_k(score, min(N_SEL, nb))
    gather = jax.vmap(jax.vmap(lambda kb, ix: kb[ix]))
    kg = gather(k_sel, idx)
    vg = gather(v_sel, idx)
    kpos = idx[..., None] * SEL_BLOCK + jnp.arange(SEL_BLOCK)
    dist = qpos[None, None, :, None, None] - kpos
    head = jnp.arange(N_HEADS).reshape(N_KV, GROUP)[None, :, :, None, None, None]
    bias = rel_bias[rel_bucket(dist)[:, :, None], head].astype(jnp.float32)
    s = jnp.einsum('bqkgd,bkqnsd->bkgqns', q, kg) + bias
    s = s.reshape(B, N_KV, GROUP, Tq, -1)
    p = masked_softmax(s, (dist >= 0).reshape(B, N_KV, 1, Tq, -1))
    o_sel = jnp.einsum('bkgqm,bkqmd->bqkgd', p, vg.reshape(B, N_KV, Tq, -1, HEAD_DIM))
    dw = qpos[:, None] - win_pos[None, :]
    s = jnp.einsum('bqkgd,blkd->bkgql', q, k_win) + head_bias(rel_bias, dw)
    p = masked_softmax(s, (dw >= 0) & (dw < WINDOW) & (win_pos[None, :] >= 0))
    o_win = jnp.einsum('bkgql,blkd->bqkgd', p, v_win)
    o = gates[..., 0:1] * o_cmp + gates[..., 1:2] * o_sel + gates[..., 2:3] * o_win
    return o.reshape(B, Tq, D_ATT)


def nsa_prompt(q, ctx, k_win, v_win, gates, rel_bias):
    B, T = q.shape[:2]
    pad = ((0, 0), (WINDOW, 0), (0, 0), (0, 0))
    kw, vw = jnp.pad(k_win, pad), jnp.pad(v_win, pad)
    span = Q_BLOCK + WINDOW

    def one_block(i):
        s0 = i * Q_BLOCK
        sl = lambda t, n: lax.dynamic_slice_in_dim(t, s0, n, axis=1)
        return nsa_block(sl(q, Q_BLOCK), s0 + jnp.arange(Q_BLOCK), *ctx, sl(kw, span), sl(vw, span),
                         s0 - WINDOW + jnp.arange(span), sl(gates, Q_BLOCK), rel_bias)

    out = lax.map(one_block, jnp.arange(T // Q_BLOCK))
    return out.transpose(1, 0, 2, 3).reshape(B, T, D_ATT)


def short_conv(u, past, w):
    T = u.shape[1]
    up = jnp.concatenate([past.astype(u.dtype), u], axis=1)
    y = w[0] * up[:, 0:T]
    for i in range(1, CONV_W):
        y = y + w[i] * up[:, i:i + T]
    return y, up[:, -(CONV_W - 1):]


def branch_inputs(h, g, w):
    B, T = h.shape[:2]
    a_b, a_c, a_h, a_z, q, kc, vc, ks, vs, kw, vw, ng, b_z, m_a, m_b = split_sections(rmsnorm(h, g) @ w)
    q = q.astype(jnp.float32).reshape(B, T, N_KV, GROUP, HEAD_DIM) * HEAD_DIM ** -0.5
    kv = tuple(t.reshape(B, T, N_KV, HEAD_DIM) for t in (kc, vc, ks, vs, kw, vw))
    gates = jax.nn.sigmoid(ng.astype(jnp.float32)).reshape(B, T, N_KV, GROUP, 3)
    return a_b, a_c * a_h, a_z, q, kv, gates, b_z, m_a, m_b


def branch_merge(h, a_b, conv_out, a_z, att, b_z, m_a, m_b, w_a_out, w_b_out, w_o):
    y_a = (a_b * conv_out * jax.nn.silu(a_z)) @ w_a_out
    y_b = (att.astype(h.dtype) * jax.nn.silu(b_z)) @ w_b_out
    return h + (jax.nn.sigmoid(m_a) * y_a + jax.nn.sigmoid(m_b) * y_b) @ w_o


def setup_inputs(seed: int = 0) -> dict:
    key = jax.random.key(seed)
    ks = jax.random.split(key, 18)
    n_pages = PAST_LEN // PAGE_SIZE
    n_pool = (DEC_BATCH * n_pages * 5) // 4
    wb = min(WINDOW, PAST_LEN)
    nrm = lambda k, shape, scale: jax.random.normal(k, shape, jnp.float32) * scale
    page_table = jax.random.permutation(ks[0], n_pool)[:DEC_BATCH * n_pages].reshape(DEC_BATCH, n_pages).astype(jnp.int32)
    return {
        'x_prompt': nrm(ks[1], (BATCH, SEQ, D_MODEL), 1.0),
        'x_sample': nrm(ks[2], (DEC_BATCH, DEC_SEQ, D_MODEL), 1.0),
        'cache_kv': nrm(ks[3], (n_pool, DEPTH, PAGE_SIZE, 4, N_KV, HEAD_DIM), 1.0),
        'cache_win': nrm(ks[4], (DEC_BATCH, DEPTH, wb, 2, N_KV, HEAD_DIM), 1.0),
        'state_conv': nrm(ks[5], (DEC_BATCH, DEPTH, CONV_W - 1, D_CONV), 1.0),
        'page_table': page_table,
        'rel_bias': nrm(ks[6], (N_BUCKETS, N_HEADS), 0.5),
        'norm_g': 1.0 + nrm(ks[7], (DEPTH, D_MODEL), 0.02),
        'w_in': nrm(ks[8], (DEPTH, D_MODEL, D_IN), D_MODEL ** -0.5),
        'conv_w': nrm(ks[9], (DEPTH, CONV_W, D_CONV), CONV_W ** -0.5),
        'w_cmp': nrm(ks[10], (DEPTH, 2, CMP_BLOCK, HEAD_DIM, HEAD_DIM), (CMP_BLOCK * HEAD_DIM) ** -0.5),
        'pe_cmp': nrm(ks[11], (DEPTH, 2, CMP_BLOCK, HEAD_DIM), 0.1),
        'w_a_out': nrm(ks[12], (DEPTH, D_CONV, D_MODEL), D_CONV ** -0.5),
        'w_b_out': nrm(ks[13], (DEPTH, D_ATT, D_MODEL), D_ATT ** -0.5),
        'w_o': nrm(ks[14], (DEPTH, D_MODEL, D_MODEL), D_MODEL ** -0.5),
        'final_g': 1.0 + nrm(ks[15], (D_MODEL,), 0.02),
    }


def reference(x_prompt, x_sample, cache_kv, cache_win, state_conv, page_table, rel_bias, norm_g, w_in, conv_w,
              w_cmp, pe_cmp, w_a_out, w_b_out, w_o, final_g):
    hp, hs = x_prompt, x_sample
    B, T = hp.shape[:2]
    DB, S = hs.shape[:2]
    P = page_table.shape[1] * cache_kv.shape[2]
    wb = cache_win.shape[2]
    qpos_s = P + jnp.arange(S)
    wpos_s = P - wb + jnp.arange(wb + S)
    kv_p, kv_s, win_p, win_s, conv_p, conv_s = [], [], [], [], [], []
    for l in range(DEPTH):
        a_b, u, a_z, q, kv, gates, b_z, m_a, m_b = branch_inputs(hp, norm_g[l], w_in[l])
        conv_out, conv_tail = short_conv(u, jnp.zeros((B, CONV_W - 1, D_CONV), u.dtype), conv_w[l])
        ctx = nsa_context(kv[0], kv[1], kv[2], kv[3], w_cmp[l], pe_cmp[l])
        att = nsa_prompt(q, ctx, kv[4], kv[5], gates, rel_bias)
        hp = branch_merge(hp, a_b, conv_out, a_z, att, b_z, m_a, m_b, w_a_out[l], w_b_out[l], w_o[l])
        kv_p.append(jnp.stack(kv[:4], axis=2))
        win_p.append(jnp.stack(kv[4:], axis=2)[:, T - min(WINDOW, T):])
        conv_p.append(conv_tail)
        a_b, u, a_z, q, kv, gates, b_z, m_a, m_b = branch_inputs(hs, norm_g[l], w_in[l])
        conv_out, conv_tail = short_conv(u, state_conv[:, l], conv_w[l])
        new_rows = jnp.stack(kv[:4], axis=2)
        past = cache_kv[page_table, l].reshape(DB, P, 4, N_KV, HEAD_DIM)
        full = jnp.concatenate([past, new_rows.astype(past.dtype)], axis=1)
        ctx = nsa_context(full[:, :, 0], full[:, :, 1], full[:, :, 2], full[:, :, 3], w_cmp[l], pe_cmp[l])
        wfull = jnp.concatenate([cache_win[:, l], jnp.stack(kv[4:], axis=2).astype(cache_win.dtype)], axis=1)
        att = nsa_block(q, qpos_s, *ctx, wfull[:, :, 0], wfull[:, :, 1], wpos_s, gates, rel_bias)
        hs = branch_merge(hs, a_b, conv_out, a_z, att, b_z, m_a, m_b, w_a_out[l], w_b_out[l], w_o[l])
        kv_s.append(new_rows)
        win_s.append(wfull[:, wfull.shape[1] - min(WINDOW, wfull.shape[1]):])
        conv_s.append(conv_tail)
    y_prompt = rmsnorm(hp, final_g)
    y_sample = rmsnorm(hs, final_g)
    return (y_prompt, y_sample, jnp.stack(kv_p, axis=1), jnp.stack(kv_s, axis=1), jnp.stack(win_p, axis=1),
            jnp.stack(win_s, axis=1), jnp.stack(conv_p, axis=1), jnp.stack(conv_s, axis=1))
```

```python
import functools
import math

import numpy as np
import jax
import jax.numpy as jnp
from jax import lax
from jax.experimental import pallas as pl
from jax.experimental.pallas import tpu as pltpu

F32 = jnp.float32
BF16 = jnp.bfloat16

D_MODEL = 1024
N_HEADS = 16
N_KV = 4
GROUP = N_HEADS // N_KV
HEAD_DIM = 64
D_ATT = N_HEADS * HEAD_DIM
D_KV = N_KV * HEAD_DIM
D_CONV = D_MODEL
CONV_W = 3
CMP_BLOCK = 32
CMP_STRIDE = 16
SEL_BLOCK = 64
N_SEL = 8
WINDOW = 512
Q_BLOCK = 128
FORCED = 1e4
N_BUCKETS = 32
MAX_DISTANCE = 128
EPS = 1e-6
NEG = -1e30
PAGE = 128
CPP = PAGE // CMP_STRIDE

COL_B, COL_C, COL_H, COL_ZA, COL_Q, COL_BZ, COL_MA, COL_MB = range(8)
D_MAIN = 8 * 1024
TN = 512
D_P = D_MAIN + 4 * TN
_O_KC, _O_NG, _O_BZ, _O_END = 5120, 6656, 6704, 9776

LANE = 128
VMEM_LIMIT = 56 * 1024 * 1024

_EXACT = N_BUCKETS // 2
_THRESH = tuple(int(math.ceil(_EXACT * (MAX_DISTANCE / _EXACT) ** (k / (N_BUCKETS - _EXACT))))
                for k in range(1, N_BUCKETS - _EXACT))


def _params(sem, vmem=VMEM_LIMIT):
    return pltpu.CompilerParams(dimension_semantics=sem, vmem_limit_bytes=vmem)


def _bucket(d):
    d = jnp.maximum(d, 0)
    big = jnp.full(d.shape, _EXACT, jnp.int32)
    for thr in _THRESH:
        big = big + (d >= thr).astype(jnp.int32)
    return jnp.where(d < _EXACT, d, big)


def _dot(a, b):
    return jnp.dot(a, b, preferred_element_type=F32)


def _dot_nt(a, b):
    return lax.dot_general(a, b, (((1,), (1,)), ((), ())), preferred_element_type=F32)


def _dot_tn(a, b):
    return lax.dot_general(a, b, (((0,), (0,)), ((), ())), preferred_element_type=F32)


def _inproj_kernel(x_ref, g_ref, w_ref, main_ref, pc_ref, ps_ref, pw_ref, pg_ref, xn_ref):
    j = pl.program_id(1)
    n_main = D_MAIN // TN

    @pl.when(j == 0)
    def _():
        x = x_ref[...]
        ms = jnp.mean(x * x, axis=-1, keepdims=True)
        xn_ref[...] = (x * lax.rsqrt(ms + EPS) * g_ref[...]).astype(BF16)

    y = _dot(xn_ref[...], w_ref[...])

    @pl.when(j < n_main)
    def _():
        main_ref[...] = y

    for k, ref in enumerate((pc_ref, ps_ref, pw_ref, pg_ref)):
        @pl.when(j == n_main + k)
        def _(ref=ref):
            ref[...] = y


def _inproj(x, g, w, tm):
    n = x.shape[0]
    n_main = D_MAIN // TN
    side = pl.BlockSpec((tm, TN), lambda i, j: (i, 0))
    side_shape = jax.ShapeDtypeStruct((n, TN), F32)
    return pl.pallas_call(
        _inproj_kernel,
        grid=(n // tm, D_P // TN),
        in_specs=[pl.BlockSpec((tm, D_MODEL), lambda i, j: (i, 0)),
                  pl.BlockSpec((1, D_MODEL), lambda i, j: (0, 0)),
                  pl.BlockSpec((D_MODEL, TN), lambda i, j: (0, j))],
        out_specs=[pl.BlockSpec((tm, TN), lambda i, j: (i, jnp.minimum(j, n_main - 1))),
                   side, side, side, side],
        out_shape=[jax.ShapeDtypeStruct((n, D_MAIN), F32), side_shape, side_shape, side_shape, side_shape],
        scratch_shapes=[pltpu.VMEM((tm, D_MODEL), BF16)],
        compiler_params=_params(("arbitrary", "arbitrary")),
        name="inproj",
    )(x, g, w)


def _silu(z):
    return z * jax.nn.sigmoid(z)


def _conv_prompt_kernel(b_ref, c_ref, h_ref, z_ref, cp_ref, hp_ref, w_ref, ga_ref, tail_ref):
    t = pl.program_id(1)
    u = c_ref[0] * h_ref[0]
    up = jnp.where(t > 0, cp_ref[0] * hp_ref[0], 0.0)
    tt = u.shape[0]
    row = lax.broadcasted_iota(jnp.int32, u.shape, 0)
    um1 = jnp.where(row == 0, up[7:8], pltpu.roll(u, 1, 0))
    um2 = jnp.where(row == 0, up[6:7], jnp.where(row == 1, up[7:8], pltpu.roll(u, 2, 0)))
    w = w_ref[...]
    conv = w[0:1] * um2 + w[1:2] * um1 + w[2:3] * u
    ga_ref[0] = (b_ref[0] * conv * _silu(z_ref[0])).astype(BF16)

    @pl.when(t == pl.num_programs(1) - 1)
    def _():
        tail_ref[0] = u[tt - 8:]


def _conv_prompt(p3, conv_w, tt):
    bsz, t, _ = p3.shape
    hb = tt // 8
    col = lambda k: (lambda b, i: (b, i, k))
    halo = lambda k: (lambda b, i: (b, jnp.maximum(i * hb - 1, 0), k))
    return pl.pallas_call(
        _conv_prompt_kernel,
        grid=(bsz, t // tt),
        in_specs=[pl.BlockSpec((1, tt, 1024), col(COL_B)),
                  pl.BlockSpec((1, tt, 1024), col(COL_C)),
                  pl.BlockSpec((1, tt, 1024), col(COL_H)),
                  pl.BlockSpec((1, tt, 1024), col(COL_ZA)),
                  pl.BlockSpec((1, 8, 1024), halo(COL_C)),
                  pl.BlockSpec((1, 8, 1024), halo(COL_H)),
                  pl.BlockSpec((CONV_W, D_CONV), lambda b, i: (0, 0))],
        out_specs=[pl.BlockSpec((1, tt, 1024), lambda b, i: (b, i, 0)),
                   pl.BlockSpec((1, 8, 1024), lambda b, i: (b, 0, 0))],
        out_shape=[jax.ShapeDtypeStruct((bsz, t, D_CONV), BF16),
                   jax.ShapeDtypeStruct((bsz, 8, D_CONV), F32)],
        compiler_params=_params(("arbitrary", "arbitrary")),
        name="conv_prompt",
    )(p3, p3, p3, p3, p3, p3, conv_w)


def _conv_sample_kernel(b_ref, c_ref, h_ref, z_ref, s1_ref, s2_ref, w_ref, ga_ref, u_ref, *, s):
    u = c_ref[...] * h_ref[...]
    t = lax.broadcasted_iota(jnp.int32, u.shape, 0) % s
    um1 = jnp.where(t >= 1, pltpu.roll(u, 1, 0), s1_ref[...])
    um2 = jnp.where(t >= 2, pltpu.roll(u, 2, 0), s2_ref[...])
    w = w_ref[...]
    conv = w[0:1] * um2 + w[1:2] * um1 + w[2:3] * u
    ga_ref[...] = (b_ref[...] * conv * _silu(z_ref[...])).astype(BF16)
    u_ref[...] = u


def _conv_sample(p2, s1, s2, conv_w, s):
    n = p2.shape[0]
    col = lambda k: pl.BlockSpec((n, 1024), lambda i: (0, k))
    full = pl.BlockSpec((n, 1024), lambda i: (0, 0))
    return pl.pallas_call(
        functools.partial(_conv_sample_kernel, s=s),
        grid=(1,),
        in_specs=[col(COL_B), col(COL_C), col(COL_H), col(COL_ZA), full, full,
                  pl.BlockSpec((CONV_W, D_CONV), lambda i: (0, 0))],
        out_specs=[full, full],
        out_shape=[jax.ShapeDtypeStruct((n, D_CONV), BF16),
                   jax.ShapeDtypeStruct((n, D_CONV), F32)],
        compiler_params=_params(("arbitrary",)),
        name="conv_sample",
    )(p2, p2, p2, p2, s1, s2, conv_w)


def _compress_body(get_xj, bd_ref, pe_ref, o_ref, bb, nch):
    nrow = bb * nch
    lo = jnp.zeros((nrow, D_KV), F32)
    hi = jnp.zeros((nrow, D_KV), F32)
    for j in range(CMP_STRIDE):
        xj = get_xj(j)
        lo = lo + _dot((xj + pe_ref[0, j:j + 1, :]).astype(BF16), bd_ref[0, j])
        hi = hi + _dot((xj + pe_ref[0, CMP_STRIDE + j:CMP_STRIDE + j + 1, :]).astype(BF16),
                       bd_ref[0, CMP_STRIDE + j])
    out = lo + pltpu.roll(hi, nrow - 1, 0)
    row = lax.broadcasted_iota(jnp.int32, out.shape, 0)
    out = jnp.where(row % nch == nch - 1, 0.0, out)
    for b in range(bb):
        o_ref[b] = out[b * nch:(b + 1) * nch]


def _compress_prompt_kernel(x_ref, bd_ref, pe_ref, o_ref, *, bb, nch):
    sec = pl.program_id(1)

    def get_xj(j):
        def pick(s_):
            off = j * TN + s_ * D_KV
            return jnp.concatenate([x_ref[b, :, off:off + D_KV] for b in range(bb)], axis=0)
        return jnp.where(sec == 0, pick(0), pick(1))

    _compress_body(get_xj, bd_ref, pe_ref, o_ref, bb, nch)


def _compress_sample_kernel(pt_ref, *refs, bb, npg):
    del pt_ref
    x_refs = refs[:bb * npg]
    bd_ref, pe_ref, o_ref = refs[bb * npg:]
    sec = pl.program_id(1)

    def get_xj(j):
        def pick(s_):
            off = j * 4 * D_KV + s_ * D_KV
            return jnp.concatenate([r[0, 0, :, off:off + D_KV] for r in x_refs], axis=0)
        return jnp.where(sec == 0, pick(0), pick(1))

    _compress_body(get_xj, bd_ref, pe_ref, o_ref, bb, npg * CPP)


def _compress_prompt(pc3, bd, pe, bb):
    bsz, nch, width = pc3.shape
    return pl.pallas_call(
        functools.partial(_compress_prompt_kernel, bb=bb, nch=nch),
        grid=(bsz // bb, 2),
        in_specs=[pl.BlockSpec((bb, nch, width), lambda i, sec: (i, 0, 0)),
                  pl.BlockSpec((1,) + bd.shape[1:], lambda i, sec: (sec, 0, 0, 0)),
                  pl.BlockSpec((1,) + pe.shape[1:], lambda i, sec: (sec, 0, 0))],
        out_specs=pl.BlockSpec((bb, nch, D_KV), lambda i, sec: (i, 0, sec)),
        out_shape=jax.ShapeDtypeStruct((bsz, nch, 2 * D_KV), F32),
        compiler_params=_params(("arbitrary", "arbitrary")),
        name="compress_prompt",
    )(pc3, bd, pe)


def _compress_sample(cache_c, pt_flat, layer, bd, pe, dbs, npg, bb):
    nch = npg * CPP
    width = cache_c.shape[-1]
    xs = [pl.BlockSpec((1, 1, CPP, width),
                       (lambda i, sec, pt, b=b, g=g: (pt[(i * bb + b) * npg + g], layer, 0, 0)))
          for b in range(bb) for g in range(npg)]
    grid_spec = pltpu.PrefetchScalarGridSpec(
        num_scalar_prefetch=1,
        grid=(dbs // bb, 2),
        in_specs=xs + [pl.BlockSpec((1,) + bd.shape[1:], lambda i, sec, pt: (sec, 0, 0, 0)),
                       pl.BlockSpec((1,) + pe.shape[1:], lambda i, sec, pt: (sec, 0, 0))],
        out_specs=pl.BlockSpec((bb, nch, D_KV), lambda i, sec, pt: (i, 0, sec)),
    )
    return pl.pallas_call(
        functools.partial(_compress_sample_kernel, bb=bb, npg=npg),
        grid_spec=grid_spec,
        out_shape=jax.ShapeDtypeStruct((dbs, nch, 2 * D_KV), F32),
        compiler_params=_params(("arbitrary", "arbitrary")),
        name="compress_sample",
    )(pt_flat, *([cache_c] * (bb * npg)), bd, pe)


def _bias_prompt_kernel(rb_ref, o_ref, *, nq):
    s = pl.program_id(0)
    r = lax.broadcasted_iota(jnp.int32, (Q_BLOCK, LANE), 0)
    c = lax.broadcasted_iota(jnp.int32, (Q_BLOCK, LANE), 1)
    d_cmp = s * Q_BLOCK + r - CMP_STRIDE * c - (CMP_BLOCK - 1)
    d = jnp.where(s < nq, d_cmp, jnp.where(s == nq, r - c, Q_BLOCK + r - c))
    bk = _bucket(d)
    for head in range(N_HEADS):
        acc = jnp.zeros((Q_BLOCK, LANE), F32)
        for b in range(N_BUCKETS):
            acc = jnp.where(bk == b, rb_ref[b, head], acc)
        g = head % GROUP
        o_ref[0, head // GROUP, g * Q_BLOCK:(g + 1) * Q_BLOCK, :] = acc


def _bias_prompt(rel_bias, nq):
    return pl.pallas_call(
        functools.partial(_bias_prompt_kernel, nq=nq),
        grid=(nq + 2,),
        in_specs=[pl.BlockSpec(memory_space=pltpu.SMEM)],
        out_specs=pl.BlockSpec((1, N_KV, GROUP * Q_BLOCK, LANE), lambda s: (s, 0, 0, 0)),
        out_shape=jax.ShapeDtypeStruct((nq + 2, N_KV, GROUP * Q_BLOCK, LANE), F32),
        compiler_params=_params(("arbitrary",)),
        name="bias_prompt",
    )(rel_bias)


def _bias_sample_kernel(rbx_ref, bc_ref, bs_ref, bw_ref, *, s, p, wb):
    def table(o_ref, dfn):
        shape = o_ref.shape
        row = lax.broadcasted_iota(jnp.int32, shape, 0)
        t = lax.broadcasted_iota(jnp.int32, shape, 1) % s
        bk = _bucket(dfn(row, t))
        acc = jnp.zeros(shape, F32)
        for b in range(N_BUCKETS):
            acc = jnp.where(bk == b, rbx_ref[b:b + 1, :], acc)
        o_ref[...] = acc

    table(bc_ref, lambda row, t: p + t - CMP_STRIDE * row - (CMP_BLOCK - 1))
    table(bs_ref, lambda row, t: p + t - row)
    table(bw_ref, lambda row, t: wb + t - row)


def _bias_sample(rbx, s, p, wb, nch, lk, lw):
    nl = N_HEADS * s
    return pl.pallas_call(
        functools.partial(_bias_sample_kernel, s=s, p=p, wb=wb),
        out_shape=[jax.ShapeDtypeStruct((nch, nl), F32),
                   jax.ShapeDtypeStruct((lk, nl), F32),
                   jax.ShapeDtypeStruct((lw, nl), F32)],
        compiler_params=pltpu.CompilerParams(vmem_limit_bytes=VMEM_LIMIT),
        name="bias_sample",
    )(rbx)


def _attn_prompt_kernel(c31_ref, q_ref, ng_ref, kcv_ref, kvs_ref, kvw_ref, tc_ref, t01_ref, c2s_ref, e_ref,
                        o_ref, selm_ref, m_ref, l_ref, acc_ref, *, nc, nb):
    i = pl.program_id(1)
    rows = GROUP * Q_BLOCK
    row = lax.broadcasted_iota(jnp.int32, (rows, LANE), 0)
    col = lax.broadcasted_iota(jnp.int32, (rows, LANE), 1)
    rq = row % Q_BLOCK
    dc = i * Q_BLOCK + rq - CMP_STRIDE * col - (CMP_BLOCK - 1)
    mask_c = (dc >= 0) & (col < nc)
    causal = rq >= col
    upper = col > rq

    qp = i * Q_BLOCK + lax.broadcasted_iota(jnp.int32, (Q_BLOCK, LANE), 0)
    jj = lax.broadcasted_iota(jnp.int32, (Q_BLOCK, LANE), 1)
    cur = qp // SEL_BLOCK
    valid = (jj * SEL_BLOCK <= qp) & (jj < nb)
    forced = (jj == 0) | (jj == cur) | (jj == cur - 1)
    sig = jax.nn.sigmoid(ng_ref[0])
    n_sel = min(N_SEL, nb)

    def tile4(x):
        return jnp.concatenate([x] * GROUP, axis=0)

    for h in range(N_KV):
        qh = jnp.concatenate(
            [q_ref[0, :, (GROUP * h + g) * HEAD_DIM:(GROUP * h + g + 1) * HEAD_DIM] for g in range(GROUP)],
            axis=0)
        qh = (qh * HEAD_DIM ** -0.5).astype(BF16)
        lk = slice(h * HEAD_DIM, (h + 1) * HEAD_DIM)
        lv = slice(D_KV + h * HEAD_DIM, D_KV + (h + 1) * HEAD_DIM)

        s = _dot_nt(qh, kcv_ref[0, :, lk].astype(BF16)) + tc_ref[0, h]
        s = jnp.where(mask_c, s, NEG)
        p = jnp.where(mask_c, jnp.exp(s - jnp.max(s, axis=-1, keepdims=True)), 0.0)
        p = p / jnp.maximum(jnp.sum(p, axis=-1, keepdims=True), 1e-30)
        o_cmp = _dot(p.astype(BF16), kcv_ref[0, :, lv].astype(BF16))

        psum = p[0:Q_BLOCK]
        for g in range(1, GROUP):
            psum = psum + p[g * Q_BLOCK:(g + 1) * Q_BLOCK]
        imp = jnp.dot(psum, c2s_ref[...], preferred_element_type=F32, precision=lax.Precision.HIGHEST)
        score = jnp.where(valid, jnp.where(forced, FORCED, imp), -jnp.inf)
        rank = jnp.zeros((Q_BLOCK, LANE), jnp.int32)
        for jp in range(nb):
            cj = score[:, jp:jp + 1]
            beats = (cj > score) | ((cj == score) & (jj > jp))
            rank = rank + beats.astype(jnp.int32)
        sel = jnp.where(rank < n_sel, 1.0, 0.0).astype(BF16)
        selm_ref[...] = _dot(sel, e_ref[...])

        cb = jnp.concatenate([jnp.full((Q_BLOCK, LANE), c31_ref[GROUP * h + g], F32) for g in range(GROUP)],
                             axis=0)

        def reset():
            m_ref[...] = jnp.full(m_ref.shape, NEG, F32)
            l_ref[...] = jnp.zeros(l_ref.shape, F32)
            acc_ref[...] = jnp.zeros(acc_ref.shape, F32)

        def tile(kv_ref, kt, bias, mask):
            start = pl.multiple_of(kt * Q_BLOCK, Q_BLOCK)
            k = kv_ref[0, pl.ds(start, Q_BLOCK), lk].astype(BF16)
            v = kv_ref[0, pl.ds(start, Q_BLOCK), lv].astype(BF16)
            sc = _dot_nt(qh, k) + bias
            if mask is not None:
                sc = jnp.where(mask, sc, NEG)
            m_old = m_ref[...]
            m_new = jnp.maximum(m_old, jnp.max(sc, axis=-1, keepdims=True))
            pr = jnp.exp(sc - m_new)
            if mask is not None:
                pr = jnp.where(mask, pr, 0.0)
            alpha = jnp.exp(m_old - m_new)
            l_ref[...] = alpha * l_ref[...] + jnp.sum(pr, axis=-1, keepdims=True)
            acc_ref[...] = alpha * acc_ref[...] + _dot(pr.astype(BF16), v)
            m_ref[...] = m_new

        def selmask(kt):
            start = pl.multiple_of(kt * Q_BLOCK, Q_BLOCK)
            return tile4(selm_ref[:, pl.ds(start, Q_BLOCK)]) > 0.5

        def finish():
            return acc_ref[...] / jnp.maximum(l_ref[...], 1e-30)

        reset()

        def far_body(kt, carry):
            tile(kvs_ref, kt, cb, selmask(kt))
            return carry

        lax.fori_loop(0, jnp.maximum(i - 1, 0), far_body, 0)

        @pl.when(i >= 1)
        def _():
            tile(kvs_ref, i - 1, t01_ref[1, h], selmask(i - 1))

        tile(kvs_ref, i, t01_ref[0, h], selmask(i) & causal)
        o_sel = finish()

        reset()

        @pl.when(i >= 4)
        def _():
            tile(kvw_ref, i - 4, cb, upper)

        for off in (3, 2):
            @pl.when(i >= off)
            def _(off=off):
                tile(kvw_ref, i - off, cb, None)

        @pl.when(i >= 1)
        def _():
            tile(kvw_ref, i - 1, t01_ref[1, h], None)

        tile(kvw_ref, i, t01_ref[0, h], causal)
        o_win = finish()

        for g in range(GROUP):
            head = GROUP * h + g
            rs = slice(g * Q_BLOCK, (g + 1) * Q_BLOCK)
            o = (sig[:, head:head + 1] * o_cmp[rs] + sig[:, N_HEADS + head:N_HEADS + head + 1] * o_sel[rs]
                 + sig[:, 2 * N_HEADS + head:2 * N_HEADS + head + 1] * o_win[rs])
            o_ref[0, :, head * HEAD_DIM:(head + 1) * HEAD_DIM] = o


def _attn_prompt(pm3, png3, ps3, pw3, kcv, tabs, c31, c2s, emat):
    bsz, t, _ = pm3.shape
    nq = t // Q_BLOCK
    nch = kcv.shape[1]
    nb = -(-t // SEL_BLOCK)
    rows = GROUP * Q_BLOCK
    return pl.pallas_call(
        functools.partial(_attn_prompt_kernel, nc=nch - 1, nb=nb),
        grid=(bsz, nq),
        in_specs=[pl.BlockSpec(memory_space=pltpu.SMEM),
                  pl.BlockSpec((1, Q_BLOCK, 1024), lambda b, i: (b, i, COL_Q)),
                  pl.BlockSpec((1, Q_BLOCK, LANE), lambda b, i: (b, i, 0)),
                  pl.BlockSpec((1, nch, 2 * D_KV), lambda b, i: (b, 0, 0)),
                  pl.BlockSpec((1, t, TN), lambda b, i: (b, 0, 0)),
                  pl.BlockSpec((1, t, TN), lambda b, i: (b, 0, 0)),
                  pl.BlockSpec((1, N_KV, rows, LANE), lambda b, i: (i, 0, 0, 0)),
                  pl.BlockSpec((2, N_KV, rows, LANE), lambda b, i: (nq // 2, 0, 0, 0)),
                  pl.BlockSpec(c2s.shape, lambda b, i: (0, 0)),
                  pl.BlockSpec(emat.shape, lambda b, i: (0, 0))],
        out_specs=pl.BlockSpec((1, Q_BLOCK, D_ATT), lambda b, i: (b, i, 0)),
        out_shape=jax.ShapeDtypeStruct((bsz, t, D_ATT), F32),
        scratch_shapes=[pltpu.VMEM((Q_BLOCK, t), F32),
                        pltpu.VMEM((rows, 1), F32),
                        pltpu.VMEM((rows, 1), F32),
                        pltpu.VMEM((rows, HEAD_DIM), F32)],
        compiler_params=_params(("arbitrary", "arbitrary")),
        name="attn_prompt",
    )(c31, pm3, png3, kcv, ps3, pw3, tabs, tabs, c2s, emat)


def _softmax_rows(s, mask):
    s = jnp.where(mask, s, NEG)
    p = jnp.where(mask, jnp.exp(s - jnp.max(s, axis=0, keepdims=True)), 0.0)
    return p / jnp.maximum(jnp.sum(p, axis=0, keepdims=True), 1e-30)


def _attn_sample_kernel(pt_ref, *refs, npg, s, p, wb, nb, nbp, pad):
    del pt_ref
    pg_refs = refs[:npg]
    (qbd_ref, ngt_ref, kvn_ref, wn_ref, cw_ref, kcv_ref, bc_ref, bs_ref, bw_ref,
     c2st_ref, gsum_ref, et_ref, o_ref) = refs[npg:]
    nl = N_HEADS * s
    nch = kcv_ref.shape[1]
    lk = p + pad
    lw = wb + pad
    n_sel = min(N_SEL, nb)
    qbd = (qbd_ref[0] * HEAD_DIM ** -0.5).astype(BF16)

    def lane_t(shape):
        return lax.broadcasted_iota(jnp.int32, shape, 1) % s

    def rows_of(shape):
        return lax.broadcasted_iota(jnp.int32, shape, 0)

    def pad_new(x):
        return jnp.concatenate([x, jnp.zeros((pad - s, x.shape[1]), x.dtype)], axis=0)

    sc = _dot(kcv_ref[0, :, 0:D_KV].astype(BF16), qbd) + bc_ref[...]
    c_row = rows_of((nch, nl))
    dc = p + lane_t((nch, nl)) - CMP_STRIDE * c_row - (CMP_BLOCK - 1)
    pc = _softmax_rows(sc, (dc >= 0) & (c_row < nch - 1))
    o_c = _dot_tn(pc.astype(BF16), kcv_ref[0, :, D_KV:2 * D_KV].astype(BF16))

    imp = jnp.dot(c2st_ref[...], pc, preferred_element_type=F32, precision=lax.Precision.HIGHEST)
    imp = jnp.dot(imp, gsum_ref[...], preferred_element_type=F32, precision=lax.Precision.HIGHEST)
    j_row = rows_of((nbp, nl))
    qpos = p + lane_t((nbp, nl))
    cur = qpos // SEL_BLOCK
    valid = (j_row * SEL_BLOCK <= qpos) & (j_row < nb)
    forced = (j_row == 0) | (j_row == cur) | (j_row == cur - 1)
    score = jnp.where(valid, jnp.where(forced, FORCED, imp), -jnp.inf)
    rank = jnp.zeros((nbp, nl), jnp.int32)
    for jp in range(nb):
        rj = score[jp:jp + 1, :]
        beats = (rj > score) | ((rj == score) & (j_row > jp))
        rank = rank + beats.astype(jnp.int32)
    sel = jnp.where(rank < n_sel, 1.0, 0.0).astype(BF16)
    maskf = _dot(et_ref[...], sel)

    k_all = jnp.concatenate([r[0, 0, :, 0:D_KV] for r in pg_refs] + [pad_new(kvn_ref[0, :, 0:D_KV])], axis=0)
    v_all = jnp.concatenate([r[0, 0, :, D_KV:2 * D_KV] for r in pg_refs]
                            + [pad_new(kvn_ref[0, :, D_KV:2 * D_KV])], axis=0)
    ss = _dot(k_all.astype(BF16), qbd) + bs_ref[...]
    ds_ = p + lane_t((lk, nl)) - rows_of((lk, nl))
    ps = _softmax_rows(ss, (maskf > 0.5) & (ds_ >= 0))
    o_s = _dot_tn(ps.astype(BF16), v_all.astype(BF16))

    kw = jnp.concatenate([cw_ref[0, 0, :, 0:D_KV], pad_new(wn_ref[0, :, 0:D_KV])], axis=0)
    vw = jnp.concatenate([cw_ref[0, 0, :, D_KV:2 * D_KV], pad_new(wn_ref[0, :, D_KV:2 * D_KV])], axis=0)
    sw = _dot(kw.astype(BF16), qbd) + bw_ref[...]
    w_row = rows_of((lw, nl))
    dw = wb + lane_t((lw, nl)) - w_row
    pw = _softmax_rows(sw, (dw >= 0) & (dw < WINDOW) & (p - wb + w_row >= 0))
    o_w = _dot_tn(pw.astype(BF16), vw.astype(BF16))

    g = jax.nn.sigmoid(ngt_ref[0])
    o_ref[0] = g[:, 0:1] * o_c + g[:, 1:2] * o_s + g[:, 2:3] * o_w


def _attn_sample(cache4, pt_flat, layer, qbd, ngt, ps3, pw3, cwin4, kcv, bc, bs, bw, c2st, gsum, et,
                 npg, s, p, wb, nb, nbp, pad):
    dbs = qbd.shape[0]
    nl = N_HEADS * s
    nch = kcv.shape[1]
    pgs = [pl.BlockSpec((1, 1, PAGE, 2 * D_KV), (lambda b, pt, g=g: (pt[b * npg + g], layer, 0, 1)))
           for g in range(npg)]
    const2 = lambda a: pl.BlockSpec(a.shape, lambda b, pt: (0, 0))
    grid_spec = pltpu.PrefetchScalarGridSpec(
        num_scalar_prefetch=1,
        grid=(dbs,),
        in_specs=pgs + [
            pl.BlockSpec((1, D_KV, nl), lambda b, pt: (b, 0, 0)),
            pl.BlockSpec((1, nl, LANE), lambda b, pt: (b, 0, 0)),
            pl.BlockSpec((1, s, TN), lambda b, pt: (b, 0, 0)),
            pl.BlockSpec((1, s, TN), lambda b, pt: (b, 0, 0)),
            pl.BlockSpec((1, 1, wb, 2 * D_KV), lambda b, pt: (b, layer, 0, 0)),
            pl.BlockSpec((1, nch, 2 * D_KV), lambda b, pt: (b, 0, 0)),
            const2(bc), const2(bs), const2(bw), const2(c2st), const2(gsum), const2(et)],
        out_specs=pl.BlockSpec((1, nl, D_KV), lambda b, pt: (b, 0, 0)),
    )
    return pl.pallas_call(
        functools.partial(_attn_sample_kernel, npg=npg, s=s, p=p, wb=wb, nb=nb, nbp=nbp, pad=pad),
        grid_spec=grid_spec,
        out_shape=jax.ShapeDtypeStruct((dbs, nl, D_KV), F32),
        compiler_params=_params(("arbitrary",)),
        name="attn_sample",
    )(pt_flat, *([cache4] * npg), qbd, ngt, ps3, pw3, cwin4, kcv, bc, bs, bw, c2st, gsum, et)


def _merge_kernel(h_ref, ga_ref, att_ref, bz_ref, ma_ref, mb_ref, wa_ref, wb_ref, wo_ref, fg_ref, o_ref, *, final):
    y_a = _dot(ga_ref[...], wa_ref[...])
    y_b = _dot((att_ref[...] * _silu(bz_ref[...])).astype(BF16), wb_ref[...])
    mix = jax.nn.sigmoid(ma_ref[...]) * y_a + jax.nn.sigmoid(mb_ref[...]) * y_b
    out = h_ref[...] + _dot(mix.astype(BF16), wo_ref[...])
    if final:
        ms = jnp.mean(out * out, axis=-1, keepdims=True)
        out = out * lax.rsqrt(ms + EPS) * fg_ref[...]
    o_ref[...] = out


def _merge(h, ga, att, pm, wa, wb, wo, fg, tm, final):
    n = h.shape[0]
    row = pl.BlockSpec((tm, 1024), lambda i: (i, 0))
    col = lambda k: pl.BlockSpec((tm, 1024), lambda i: (i, k))
    wsp = pl.BlockSpec((1024, 1024), lambda i: (0, 0))
    return pl.pallas_call(
        functools.partial(_merge_kernel, final=final),
        grid=(n // tm,),
        in_specs=[row, row, row, col(COL_BZ), col(COL_MA), col(COL_MB),
                  wsp, wsp, wsp, pl.BlockSpec((1, 1024), lambda i: (0, 0))],
        out_specs=row,
        out_shape=jax.ShapeDtypeStruct((n, D_MODEL), F32),
        compiler_params=_params(("arbitrary",)),
        name="merge",
    )(h, ga, att, pm, pm, pm, wa, wb, wo, fg)


def _cmp_to_sel(nc, nb):
    c0 = np.arange(nc)[:, None] * CMP_STRIDE
    j0 = np.arange(nb)[None, :] * SEL_BLOCK
    ov = np.minimum(c0 + CMP_BLOCK, j0 + SEL_BLOCK) - np.maximum(c0, j0)
    return np.maximum(ov, 0).astype(np.float32) / CMP_BLOCK


def _largest_divisor(n, cap):
    for d in range(min(n, cap), 0, -1):
        if n % d == 0:
            return d
    return 1


def kernel(x_prompt, x_sample, cache_kv, cache_win, state_conv, page_table, rel_bias, norm_g, w_in, conv_w,
           w_cmp, pe_cmp, w_a_out, w_b_out, w_o, final_g):
    bsz, t, _ = x_prompt.shape
    dbs, s, _ = x_sample.shape
    depth = w_in.shape[0]
    n_pool = cache_kv.shape[0]
    npg = page_table.shape[1]
    p = npg * PAGE
    wb = cache_win.shape[2]
    nq = t // Q_BLOCK
    nch_p = t // CMP_STRIDE
    nch_s = (p + s) // CMP_STRIDE
    assert t % Q_BLOCK == 0 and nq % 2 == 0 and nch_p == LANE, "prompt length not supported"
    assert 2 <= s <= 8 and wb == min(WINDOW, p) and nch_s == p // CMP_STRIDE, "sample shape not supported"
    nb_p = -(-t // SEL_BLOCK)
    nb_s = -(-(p + s) // SEL_BLOCK)
    nbp_s = -(-nb_s // 8) * 8
    pad = 16
    lk, lw = p + pad, wb + pad
    nl = N_HEADS * s

    ng_perm = np.array([_O_NG + head * 3 + br for br in range(3) for head in range(N_HEADS)])
    perm = np.concatenate([np.arange(0, _O_KC), np.arange(_O_BZ, _O_END), np.arange(_O_KC, _O_NG), ng_perm])
    w_p = jnp.pad(w_in[:, :, perm], ((0, 0), (0, 0), (0, D_P - perm.size))).astype(BF16)
    wa_b, wb_b, wo_b = w_a_out.astype(BF16), w_b_out.astype(BF16), w_o.astype(BF16)
    eye_kv = jnp.eye(N_KV, dtype=F32)
    bd = jnp.einsum('lsjde,hk->lsjhdke', w_cmp, eye_kv).reshape(depth, 2, CMP_BLOCK, D_KV, D_KV).astype(BF16)
    pe_t = jnp.tile(pe_cmp, (1, 1, 1, N_KV))
    g2 = norm_g.reshape(depth, 1, D_MODEL)
    fg2 = final_g.reshape(1, D_MODEL)

    tabs = _bias_prompt(rel_bias, nq)
    c31 = rel_bias[N_BUCKETS - 1]
    rbx = jnp.repeat(rel_bias, s, axis=1)
    bc, bs, bw = _bias_sample(rbx, s, p, wb, nch_s, lk, lw)
    c2s_p = np.zeros((LANE, LANE), np.float32)
    c2s_p[:nch_p - 1, :nb_p] = _cmp_to_sel(nch_p - 1, nb_p)
    emat_p = (np.arange(t)[None, :] // SEL_BLOCK == np.arange(LANE)[:, None])
    c2st_s = np.zeros((nbp_s, nch_s), np.float32)
    c2st_s[:nb_s, :nch_s - 1] = _cmp_to_sel(nch_s - 1, nb_s).T
    lane = np.arange(nl)
    gsum = ((lane[:, None] // (GROUP * s) == lane[None, :] // (GROUP * s))
            & (lane[:, None] % s == lane[None, :] % s)).astype(np.float32)
    et_s = (np.arange(lk)[:, None] // SEL_BLOCK == np.arange(nbp_s)[None, :])
    c2s_p, c2st_s, gsum = jnp.asarray(c2s_p), jnp.asarray(c2st_s), jnp.asarray(gsum)
    emat_p, et_s = jnp.asarray(emat_p, dtype=BF16), jnp.asarray(et_s, dtype=BF16)

    cache4 = cache_kv.reshape(n_pool, depth, PAGE, 4 * D_KV)
    cache_c = cache_kv.reshape(n_pool, depth, CPP, CMP_STRIDE * 4 * D_KV)
    cwin4 = cache_win.reshape(dbs, depth, wb, 2 * D_KV)
    pt_flat = page_table.reshape(-1).astype(jnp.int32)

    tm_p = _largest_divisor(bsz * t, 1024)
    tt = _largest_divisor(t, 512)
    bb_p = _largest_divisor(bsz, 2)
    bb_s = _largest_divisor(dbs, 2)
    tm_m = _largest_divisor(bsz * t, 512)

    hp = x_prompt.reshape(bsz * t, D_MODEL)
    hs = x_sample.reshape(dbs * s, D_MODEL)
    kv_p, kv_s, win_p, win_s, conv_p, conv_s = [], [], [], [], [], []
    zs = jnp.zeros((dbs, s - 1, D_CONV), F32)
    for l in range(depth):
        final = l == depth - 1
        pm, pc, ps, pw, png = _inproj(hp, g2[l], w_p[l], tm_p)
        pm3 = pm.reshape(bsz, t, D_MAIN)
        ga, tail = _conv_prompt(pm3, conv_w[l], tt)
        kcv = _compress_prompt(pc.reshape(bsz, nch_p, CMP_STRIDE * TN), bd[l], pe_t[l], bb_p)
        att = _attn_prompt(pm3, png.reshape(bsz, t, TN), ps.reshape(bsz, t, TN), pw.reshape(bsz, t, TN),
                           kcv, tabs, c31, c2s_p, emat_p)
        hp = _merge(hp, ga.reshape(bsz * t, D_CONV), att.reshape(bsz * t, D_ATT), pm, wa_b[l], wb_b[l], wo_b[l],
                    fg2, tm_m, final)
        kv_p.append(jnp.concatenate([pc, ps], axis=-1).reshape(bsz, t, 4, N_KV, HEAD_DIM))
        win_p.append(pw.reshape(bsz, t, 2, N_KV, HEAD_DIM)[:, t - min(WINDOW, t):])
        conv_p.append(tail[:, 8 - (CONV_W - 1):])

        pm_s, pc_s, ps_s, pw_s, png_s = _inproj(hs, g2[l], w_p[l], dbs * s)
        st = state_conv[:, l]
        s1 = jnp.concatenate([st[:, 1:2], zs], axis=1).reshape(dbs * s, D_CONV)
        s2 = jnp.concatenate([st, zs[:, :s - 2]], axis=1).reshape(dbs * s, D_CONV)
        ga_s, u_s = _conv_sample(pm_s, s1, s2, conv_w[l], s)
        kcv_s = _compress_sample(cache_c, pt_flat, l, bd[l], pe_t[l], dbs, npg, bb_s)
        q_s = pm_s[:, COL_Q * 1024:(COL_Q + 1) * 1024].reshape(dbs, s, N_KV, GROUP, HEAD_DIM)
        q_t = q_s.transpose(0, 2, 4, 3, 1).reshape(dbs, N_KV, HEAD_DIM, GROUP * s)
        qbd = jnp.einsum('bkdm,kj->bkdjm', q_t, eye_kv).reshape(dbs, D_KV, nl)
        ngt = png_s[:, :3 * N_HEADS].reshape(dbs, s, 3, N_HEADS).transpose(0, 3, 1, 2)
        ngt = jnp.pad(ngt.reshape(dbs, nl, 3), ((0, 0), (0, 0), (0, LANE - 3)))
        o_s = _attn_sample(cache4, pt_flat, l, qbd, ngt, ps_s.reshape(dbs, s, TN), pw_s.reshape(dbs, s, TN),
                           cwin4, kcv_s, bc, bs, bw, c2st_s, gsum, et_s, npg, s, p, wb, nb_s, nbp_s, pad)
        o6 = o_s.reshape(dbs, N_KV, GROUP, s, N_KV, HEAD_DIM)
        att_s = jnp.stack([o6[:, k, :, :, k] for k in range(N_KV)], axis=1)
        att_s = att_s.transpose(0, 3, 1, 2, 4).reshape(dbs * s, D_ATT)
        hs = _merge(hs, ga_s, att_s, pm_s, wa_b[l], wb_b[l], wo_b[l], fg2, dbs * s, final)
        kv_s.append(jnp.concatenate([pc_s, ps_s], axis=-1).reshape(dbs, s, 4, N_KV, HEAD_DIM))
        new_win = pw_s.reshape(dbs, s, 2, N_KV, HEAD_DIM)
        wfull = jnp.concatenate([cache_win[:, l], new_win.astype(cache_win.dtype)], axis=1)
        win_s.append(wfull[:, wfull.shape[1] - min(WINDOW, wfull.shape[1]):])
        conv_s.append(u_s.reshape(dbs, s, D_CONV)[:, s - (CONV_W - 1):])

    y_prompt = hp.reshape(bsz, t, D_MODEL)
    y_sample = hs.reshape(dbs, s, D_MODEL)
    return (y_prompt, y_sample, jnp.stack(kv_p, axis=1), jnp.stack(kv_s, axis=1), jnp.stack(win_p, axis=1),
            jnp.stack(win_s, axis=1), jnp.stack(conv_p, axis=1), jnp.stack(conv_s, axis=1))
```

```python
import functools
import math

import numpy as np
import jax
import jax.numpy as jnp
from jax import lax
from jax.experimental import pallas as pl
from jax.experimental.pallas import tpu as pltpu

F32 = jnp.float32
BF16 = jnp.bfloat16

D_MODEL = 1024
N_HEADS = 16
N_KV = 4
GROUP = N_HEADS // N_KV
HEAD_DIM = 64
D_ATT = N_HEADS * HEAD_DIM
D_KV = N_KV * HEAD_DIM
D_CONV = D_MODEL
CONV_W = 3
CMP_BLOCK = 32
CMP_STRIDE = 16
SEL_BLOCK = 64
N_SEL = 8
WINDOW = 512
Q_BLOCK = 128
FORCED = 1e4
N_BUCKETS = 32
MAX_DISTANCE = 128
EPS = 1e-6
NEG = -1e30
PAGE = 128
CPP = PAGE // CMP_STRIDE

LANE = 128
TN = 512
COL_B, COL_C, COL_H, COL_ZA, COL_MA, COL_MB = range(6)
D_NAT = 6 * 1024
D_TR = 8 * TN
NG_ROWS = 64
_O_Q, _O_KC, _O_NG, _O_BZ, _O_MA, _O_END = 4096, 5120, 6656, 6704, 7728, 9776

VMEM_LIMIT = 56 * 1024 * 1024
XS_PITCH = 24

_EXACT = N_BUCKETS // 2
_THRESH = tuple(int(math.ceil(_EXACT * (MAX_DISTANCE / _EXACT) ** (k / (N_BUCKETS - _EXACT))))
                for k in range(1, N_BUCKETS - _EXACT))


def _params(sem, vmem=VMEM_LIMIT):
    return pltpu.CompilerParams(dimension_semantics=sem, vmem_limit_bytes=vmem)


def _bucket(d):
    d = jnp.maximum(d, 0)
    big = jnp.full(d.shape, _EXACT, jnp.int32)
    for thr in _THRESH:
        big = big + (d >= thr).astype(jnp.int32)
    return jnp.where(d < _EXACT, d, big)


def _dot(a, b):
    return jnp.dot(a, b, preferred_element_type=F32)


def _dot_nt(a, b):
    return lax.dot_general(a, b, (((1,), (1,)), ((), ())), preferred_element_type=F32)


def _dot_tn(a, b):
    return lax.dot_general(a, b, (((0,), (0,)), ((), ())), preferred_element_type=F32)


def _dot_hi(a, b):
    return jnp.dot(a, b, preferred_element_type=F32, precision=lax.Precision.HIGHEST)


def _silu(z):
    return z * jax.nn.sigmoid(z)


def _inproj_kernel(x_ref, g_ref, w_ref, wt_ref, main_ref, qt_ref, bzt_ref, kvt_ref, ngt_ref, xn_ref, *, n_nat):
    j = pl.program_id(1)
    ntile = qt_ref.shape[0]

    @pl.when(j == 0)
    def _():
        x = x_ref[...]
        ms = jnp.mean(x * x, axis=-1, keepdims=True)
        xn_ref[...] = (x * lax.rsqrt(ms + EPS) * g_ref[...]).astype(BF16)

    @pl.when(j < n_nat)
    def _():
        main_ref[...] = _dot(xn_ref[...], w_ref[...])

    @pl.when(j >= n_nat)
    def _():
        yt = _dot_nt(wt_ref[...], xn_ref[...])

        def put(ref, rows):
            for k in range(ntile):
                ref[k] = yt[:rows, k * LANE:(k + 1) * LANE]

        jt = j - n_nat

        @pl.when(jt < 2)
        def _():
            put(qt_ref, TN)

        @pl.when((jt >= 2) & (jt < 4))
        def _():
            put(bzt_ref, TN)

        @pl.when((jt >= 4) & (jt < 7))
        def _():
            put(kvt_ref, TN)

        @pl.when(jt == 7)
        def _():
            put(ngt_ref, NG_ROWS)


def _inproj(x, g, w_nat, w_tr, tm):
    n = x.shape[0]
    n_nat = D_NAT // TN
    nt = n // LANE
    tl = tm // LANE
    clampi = lambda v, lo, hi: jnp.minimum(jnp.maximum(v, lo), hi)
    return pl.pallas_call(
        functools.partial(_inproj_kernel, n_nat=n_nat),
        grid=(n // tm, n_nat + D_TR // TN),
        in_specs=[pl.BlockSpec((tm, D_MODEL), lambda i, j: (i, 0)),
                  pl.BlockSpec((1, D_MODEL), lambda i, j: (0, 0)),
                  pl.BlockSpec((D_MODEL, TN), lambda i, j: (0, jnp.minimum(j, n_nat - 1))),
                  pl.BlockSpec((TN, D_MODEL), lambda i, j: (jnp.maximum(j - n_nat, 0), 0))],
        out_specs=[pl.BlockSpec((tm, TN), lambda i, j: (i, jnp.minimum(j, n_nat - 1))),
                   pl.BlockSpec((tl, TN, LANE), lambda i, j: (i, clampi(j - n_nat, 0, 1), 0)),
                   pl.BlockSpec((tl, TN, LANE), lambda i, j: (i, clampi(j - n_nat - 2, 0, 1), 0)),
                   pl.BlockSpec((tl, TN, LANE), lambda i, j: (i, clampi(j - n_nat - 4, 0, 2), 0)),
                   pl.BlockSpec((tl, NG_ROWS, LANE), lambda i, j: (i, 0, 0))],
        out_shape=[jax.ShapeDtypeStruct((n, D_NAT), F32),
                   jax.ShapeDtypeStruct((nt, D_ATT, LANE), F32),
                   jax.ShapeDtypeStruct((nt, D_ATT, LANE), F32),
                   jax.ShapeDtypeStruct((nt, 6 * D_KV, LANE), F32),
                   jax.ShapeDtypeStruct((nt, NG_ROWS, LANE), F32)],
        scratch_shapes=[pltpu.VMEM((tm, D_MODEL), BF16)],
        compiler_params=_params(("arbitrary", "arbitrary")),
        name="inproj",
    )(x, g, w_nat, w_tr)


def _conv_prompt_kernel(b_ref, c_ref, h_ref, z_ref, cp_ref, hp_ref, w_ref, ga_ref, tail_ref):
    t = pl.program_id(1)
    u = c_ref[0] * h_ref[0]
    up = jnp.where(t > 0, cp_ref[0] * hp_ref[0], 0.0)
    tt = u.shape[0]
    row = lax.broadcasted_iota(jnp.int32, u.shape, 0)
    um1 = jnp.where(row == 0, up[7:8], pltpu.roll(u, 1, 0))
    um2 = jnp.where(row == 0, up[6:7], jnp.where(row == 1, up[7:8], pltpu.roll(u, 2, 0)))
    w = w_ref[...]
    conv = w[0:1] * um2 + w[1:2] * um1 + w[2:3] * u
    ga_ref[0] = (b_ref[0] * conv * _silu(z_ref[0])).astype(BF16)

    @pl.when(t == pl.num_programs(1) - 1)
    def _():
        tail_ref[0] = u[tt - 8:]


def _conv_prompt(p3, conv_w, tt):
    bsz, t, _ = p3.shape
    hb = tt // 8
    col = lambda k: (lambda b, i: (b, i, k))
    halo = lambda k: (lambda b, i: (b, jnp.maximum(i * hb - 1, 0), k))
    return pl.pallas_call(
        _conv_prompt_kernel,
        grid=(bsz, t // tt),
        in_specs=[pl.BlockSpec((1, tt, 1024), col(COL_B)),
                  pl.BlockSpec((1, tt, 1024), col(COL_C)),
                  pl.BlockSpec((1, tt, 1024), col(COL_H)),
                  pl.BlockSpec((1, tt, 1024), col(COL_ZA)),
                  pl.BlockSpec((1, 8, 1024), halo(COL_C)),
                  pl.BlockSpec((1, 8, 1024), halo(COL_H)),
                  pl.BlockSpec((CONV_W, D_CONV), lambda b, i: (0, 0))],
        out_specs=[pl.BlockSpec((1, tt, 1024), lambda b, i: (b, i, 0)),
                   pl.BlockSpec((1, 8, 1024), lambda b, i: (b, 0, 0))],
        out_shape=[jax.ShapeDtypeStruct((bsz, t, D_CONV), BF16),
                   jax.ShapeDtypeStruct((bsz, 8, D_CONV), F32)],
        compiler_params=_params(("arbitrary", "arbitrary")),
        name="conv_prompt",
    )(p3, p3, p3, p3, p3, p3, conv_w)


def _conv_sample_kernel(b_ref, c_ref, h_ref, z_ref, s1_ref, s2_ref, w_ref, ga_ref, u_ref, *, s):
    u = c_ref[...] * h_ref[...]
    t = lax.broadcasted_iota(jnp.int32, u.shape, 0) % s
    um1 = jnp.where(t >= 1, pltpu.roll(u, 1, 0), s1_ref[...])
    um2 = jnp.where(t >= 2, pltpu.roll(u, 2, 0), s2_ref[...])
    w = w_ref[...]
    conv = w[0:1] * um2 + w[1:2] * um1 + w[2:3] * u
    ga_ref[...] = (b_ref[...] * conv * _silu(z_ref[...])).astype(BF16)
    u_ref[...] = u


def _conv_sample(p2, s1, s2, conv_w, s):
    n = p2.shape[0]
    col = lambda k: pl.BlockSpec((n, 1024), lambda i: (0, k))
    full = pl.BlockSpec((n, 1024), lambda i: (0, 0))
    return pl.pallas_call(
        functools.partial(_conv_sample_kernel, s=s),
        grid=(1,),
        in_specs=[col(COL_B), col(COL_C), col(COL_H), col(COL_ZA), full, full,
                  pl.BlockSpec((CONV_W, D_CONV), lambda i: (0, 0))],
        out_specs=[full, full],
        out_shape=[jax.ShapeDtypeStruct((n, D_CONV), BF16),
                   jax.ShapeDtypeStruct((n, D_CONV), F32)],
        compiler_params=_params(("arbitrary",)),
        name="conv_sample",
    )(p2, p2, p2, p2, s1, s2, conv_w)


def _compress_core(get_page, npg, xs_ref, bd_ref, pe_ref):
    nch = npg * CPP
    outs = []
    for sec in range(2):
        for pg in range(npg):
            x = get_page(sec, pg).T
            for half in range(2):
                for c in range(CPP):
                    r0 = (pg * CPP + c) * XS_PITCH
                    xs_ref[half, r0:r0 + CMP_STRIDE, :] = x[c * CMP_STRIDE:(c + 1) * CMP_STRIDE,
                                                            half * LANE:(half + 1) * LANE]
        lo = jnp.zeros((nch, D_KV), F32)
        hi = jnp.zeros((nch, D_KV), F32)
        for j in range(CMP_STRIDE):
            xj = jnp.concatenate([xs_ref[half, pl.ds(j, nch, stride=XS_PITCH), :] for half in range(2)], axis=1)
            lo = lo + _dot((xj + pe_ref[sec, j:j + 1, :]).astype(BF16), bd_ref[sec, j])
            hi = hi + _dot((xj + pe_ref[sec, CMP_STRIDE + j:CMP_STRIDE + j + 1, :]).astype(BF16),
                           bd_ref[sec, CMP_STRIDE + j])
        out = lo + pltpu.roll(hi, nch - 1, 0)
        row = lax.broadcasted_iota(jnp.int32, out.shape, 0)
        out = jnp.where(row == nch - 1, 0.0, out)
        outs.append(out.T)
    return outs


def _compress_prompt_kernel(kv_ref, bd_ref, pe_ref, o_ref, xs_ref):
    npg = kv_ref.shape[0]
    kct, vct = _compress_core(lambda sec, pg: kv_ref[pg, sec * D_KV:(sec + 1) * D_KV, :], npg, xs_ref, bd_ref, pe_ref)
    o_ref[0, 0:D_KV, :] = kct
    o_ref[0, D_KV:2 * D_KV, :] = vct


def _compress_prompt(kvt, bd, pe, bsz, nq):
    nch = nq * CPP
    return pl.pallas_call(
        _compress_prompt_kernel,
        grid=(bsz,),
        in_specs=[pl.BlockSpec((nq, 2 * D_KV, LANE), lambda b: (b, 0, 0)),
                  pl.BlockSpec(bd.shape, lambda b: (0, 0, 0, 0)),
                  pl.BlockSpec(pe.shape, lambda b: (0, 0, 0))],
        out_specs=pl.BlockSpec((1, 2 * D_KV, nch), lambda b: (b, 0, 0)),
        out_shape=jax.ShapeDtypeStruct((bsz, 2 * D_KV, nch), F32),
        scratch_shapes=[pltpu.VMEM((2, nch * XS_PITCH, LANE), F32)],
        compiler_params=_params(("arbitrary",)),
        name="compress_prompt",
    )(kvt, bd, pe)


def _bias_prompt_kernel(rb_ref, o_ref, *, nq):
    s = pl.program_id(0)
    r = lax.broadcasted_iota(jnp.int32, (Q_BLOCK, LANE), 0)
    c = lax.broadcasted_iota(jnp.int32, (Q_BLOCK, LANE), 1)
    d_cmp = s * Q_BLOCK + c - CMP_STRIDE * r - (CMP_BLOCK - 1)
    d = jnp.where(s < nq, d_cmp, jnp.where(s == nq, c - r, Q_BLOCK + c - r))
    bk = _bucket(d)
    for head in range(N_HEADS):
        acc = jnp.zeros((Q_BLOCK, LANE), F32)
        for b in range(N_BUCKETS):
            acc = jnp.where(bk == b, rb_ref[b, head], acc)
        g = head % GROUP
        o_ref[0, head // GROUP, :, g * Q_BLOCK:(g + 1) * Q_BLOCK] = acc


def _bias_prompt(rel_bias, nq):
    return pl.pallas_call(
        functools.partial(_bias_prompt_kernel, nq=nq),
        grid=(nq + 2,),
        in_specs=[pl.BlockSpec(memory_space=pltpu.SMEM)],
        out_specs=pl.BlockSpec((1, N_KV, Q_BLOCK, GROUP * Q_BLOCK), lambda s: (s, 0, 0, 0)),
        out_shape=jax.ShapeDtypeStruct((nq + 2, N_KV, Q_BLOCK, GROUP * Q_BLOCK), F32),
        compiler_params=_params(("arbitrary",)),
        name="bias_prompt",
    )(rel_bias)


def _bias_sample_kernel(rbx_ref, bc_ref, bs_ref, bw_ref, *, s, p, wb):
    def table(o_ref, dfn):
        shape = o_ref.shape
        t = lax.broadcasted_iota(jnp.int32, shape, 0) % s
        k = lax.broadcasted_iota(jnp.int32, shape, 1)
        bk = _bucket(dfn(t, k))
        acc = jnp.zeros(shape, F32)
        for b in range(N_BUCKETS):
            acc = jnp.where(bk == b, rbx_ref[:, b:b + 1], acc)
        o_ref[...] = acc

    table(bc_ref, lambda t, k: p + t - CMP_STRIDE * k - (CMP_BLOCK - 1))
    table(bs_ref, lambda t, k: p + t - k)
    table(bw_ref, lambda t, k: wb + t - k)


def _bias_sample(rbx, s, p, wb, nch, lk, lw):
    nl = N_HEADS * s
    return pl.pallas_call(
        functools.partial(_bias_sample_kernel, s=s, p=p, wb=wb),
        out_shape=[jax.ShapeDtypeStruct((nl, nch), F32),
                   jax.ShapeDtypeStruct((nl, lk), F32),
                   jax.ShapeDtypeStruct((nl, lw), F32)],
        compiler_params=pltpu.CompilerParams(vmem_limit_bytes=VMEM_LIMIT),
        name="bias_sample",
    )(rbx)


def _attn_prompt_kernel(c31_ref, qt_ref, ngt_ref, kcv_ref, ks_ref, vs_ref, kw_ref, vw_ref, tc_ref, t01_ref, c2st_ref,
                        o_ref, sel_ref, m_ref, l_ref, acc_ref, *, nc, nb):
    i = pl.program_id(1)
    width = GROUP * Q_BLOCK
    key = lax.broadcasted_iota(jnp.int32, (Q_BLOCK, width), 0)
    lq = lax.broadcasted_iota(jnp.int32, (Q_BLOCK, width), 1) % Q_BLOCK
    dc = i * Q_BLOCK + lq - CMP_STRIDE * key - (CMP_BLOCK - 1)
    mask_c = (dc >= 0) & (key < nc)
    causal = lq >= key
    upper = key > lq

    nbp = sel_ref.shape[0]
    jj = lax.broadcasted_iota(jnp.int32, (nbp, Q_BLOCK), 0)
    qp = i * Q_BLOCK + lax.broadcasted_iota(jnp.int32, (nbp, Q_BLOCK), 1)
    cur = qp // SEL_BLOCK
    valid = (jj * SEL_BLOCK <= qp) & (jj < nb)
    forced = (jj == 0) | (jj == cur) | (jj == cur - 1)
    sig = jax.nn.sigmoid(ngt_ref[0])
    n_sel = min(N_SEL, nb)
    half = SEL_BLOCK

    for h in range(N_KV):
        hs = slice(h * HEAD_DIM, (h + 1) * HEAD_DIM)
        qt = jnp.concatenate(
            [qt_ref[0, (GROUP * h + g) * HEAD_DIM:(GROUP * h + g + 1) * HEAD_DIM, :] for g in range(GROUP)],
            axis=1)
        qt = (qt * HEAD_DIM ** -0.5).astype(BF16)

        s = _dot_tn(kcv_ref[0, hs, :].astype(BF16), qt) + tc_ref[0, h]
        s = jnp.where(mask_c, s, NEG)
        p = jnp.where(mask_c, jnp.exp(s - jnp.max(s, axis=0, keepdims=True)), 0.0)
        p = p / jnp.maximum(jnp.sum(p, axis=0, keepdims=True), 1e-30)
        o_cmp = _dot(kcv_ref[0, D_KV + h * HEAD_DIM:D_KV + (h + 1) * HEAD_DIM, :].astype(BF16), p.astype(BF16))

        psum = p[:, 0:Q_BLOCK]
        for g in range(1, GROUP):
            psum = psum + p[:, g * Q_BLOCK:(g + 1) * Q_BLOCK]
        imp = _dot_hi(c2st_ref[...], psum)
        score = jnp.where(valid, jnp.where(forced, FORCED, imp), -jnp.inf)
        rank = jnp.zeros((nbp, Q_BLOCK), jnp.int32)
        for jp in range(nb):
            rj = score[jp:jp + 1, :]
            beats = (rj > score) | ((rj == score) & (jj > jp))
            rank = rank + beats.astype(jnp.int32)
        sel = jnp.where(rank < n_sel, 1.0, 0.0)
        sel_ref[...] = jnp.concatenate([sel] * GROUP, axis=1)

        cb = jnp.concatenate([jnp.full((Q_BLOCK, Q_BLOCK), c31_ref[GROUP * h + g], F32) for g in range(GROUP)],
                             axis=1)

        def reset():
            m_ref[...] = jnp.full(m_ref.shape, NEG, F32)
            l_ref[...] = jnp.zeros(l_ref.shape, F32)
            acc_ref[...] = jnp.zeros(acc_ref.shape, F32)

        def tile(k_ref, v_ref, kt, bias, mask):
            sc = _dot_tn(k_ref[kt, hs, :].astype(BF16), qt) + bias
            if mask is not None:
                sc = jnp.where(mask, sc, NEG)
            m_old = m_ref[...]
            m_new = jnp.maximum(m_old, jnp.max(sc, axis=0, keepdims=True))
            pr = jnp.exp(sc - m_new)
            if mask is not None:
                pr = jnp.where(mask, pr, 0.0)
            alpha = jnp.exp(m_old - m_new)
            l_ref[...] = alpha * l_ref[...] + jnp.sum(pr, axis=0, keepdims=True)
            acc_ref[...] = alpha * acc_ref[...] + _dot(v_ref[kt, hs, :].astype(BF16), pr.astype(BF16))
            m_ref[...] = m_new

        def selmask(kt):
            r0 = sel_ref[pl.ds(2 * kt, 1), :]
            r1 = sel_ref[pl.ds(2 * kt + 1, 1), :]
            return jnp.concatenate([jnp.broadcast_to(r0, (half, width)), jnp.broadcast_to(r1, (half, width))],
                                   axis=0) > 0.5

        def finish():
            return acc_ref[...] / jnp.maximum(l_ref[...], 1e-30)

        reset()

        def far_body(kt, carry):
            tile(ks_ref, vs_ref, kt, cb, selmask(kt))
            return carry

        lax.fori_loop(0, jnp.maximum(i - 1, 0), far_body, 0)

        @pl.when(i >= 1)
        def _():
            tile(ks_ref, vs_ref, i - 1, t01_ref[1, h], selmask(i - 1))

        tile(ks_ref, vs_ref, i, t01_ref[0, h], selmask(i) & causal)
        o_sel = finish()

        reset()

        @pl.when(i >= 4)
        def _():
            tile(kw_ref, vw_ref, i - 4, cb, upper)

        for off in (3, 2):
            @pl.when(i >= off)
            def _(off=off):
                tile(kw_ref, vw_ref, i - off, cb, None)

        @pl.when(i >= 1)
        def _():
            tile(kw_ref, vw_ref, i - 1, t01_ref[1, h], None)

        tile(kw_ref, vw_ref, i, t01_ref[0, h], causal)
        o_win = finish()

        for g in range(GROUP):
            head = GROUP * h + g
            ls = slice(g * Q_BLOCK, (g + 1) * Q_BLOCK)
            o = (sig[head:head + 1, :] * o_cmp[:, ls] + sig[N_HEADS + head:N_HEADS + head + 1, :] * o_sel[:, ls]
                 + sig[2 * N_HEADS + head:2 * N_HEADS + head + 1, :] * o_win[:, ls])
            o_ref[0, head * HEAD_DIM:(head + 1) * HEAD_DIM, :] = o


def _attn_prompt(qt, ngt, kvt, kcv, tabs, c31, c2st, bsz, nq):
    nch = kcv.shape[2]
    nb = c2st.shape[0]
    width = GROUP * Q_BLOCK
    kvspec = lambda k: pl.BlockSpec((nq, D_KV, LANE), lambda b, i: (b, k, 0))
    return pl.pallas_call(
        functools.partial(_attn_prompt_kernel, nc=nch - 1, nb=nb),
        grid=(bsz, nq),
        in_specs=[pl.BlockSpec(memory_space=pltpu.SMEM),
                  pl.BlockSpec((1, D_ATT, LANE), lambda b, i: (b * nq + i, 0, 0)),
                  pl.BlockSpec((1, NG_ROWS, LANE), lambda b, i: (b * nq + i, 0, 0)),
                  pl.BlockSpec((1, 2 * D_KV, nch), lambda b, i: (b, 0, 0)),
                  kvspec(2), kvspec(3), kvspec(4), kvspec(5),
                  pl.BlockSpec((1, N_KV, Q_BLOCK, width), lambda b, i: (i, 0, 0, 0)),
                  pl.BlockSpec((2, N_KV, Q_BLOCK, width), lambda b, i: (nq // 2, 0, 0, 0)),
                  pl.BlockSpec(c2st.shape, lambda b, i: (0, 0))],
        out_specs=pl.BlockSpec((1, D_ATT, LANE), lambda b, i: (b * nq + i, 0, 0)),
        out_shape=jax.ShapeDtypeStruct((bsz * nq, D_ATT, LANE), F32),
        scratch_shapes=[pltpu.VMEM((nb, width), F32),
                        pltpu.VMEM((1, width), F32),
                        pltpu.VMEM((1, width), F32),
                        pltpu.VMEM((HEAD_DIM, width), F32)],
        compiler_params=_params(("arbitrary", "arbitrary")),
        name="attn_prompt",
    )(c31, qt, ngt, kcv, kvt, kvt, kvt, kvt, tabs, tabs, c2st)


def _softmax_lanes(s, mask):
    s = jnp.where(mask, s, NEG)
    p = jnp.where(mask, jnp.exp(s - jnp.max(s, axis=-1, keepdims=True)), 0.0)
    return p / jnp.maximum(jnp.sum(p, axis=-1, keepdims=True), 1e-30)


def _attn_sample_kernel(pt_ref, *refs, npg, s, p, wb, nb):
    del pt_ref
    pg_refs = refs[:npg]
    (qbd_ref, ngt_ref, new_ref, cw_ref, bd_ref, pe_ref, bc_ref, bs_ref, bw_ref,
     c2s_ref, gsum_ref, e_ref, o_ref, xs_ref) = refs[npg:]
    nl = N_HEADS * s
    nch = npg * CPP
    n_sel = min(N_SEL, nb)
    qb = (qbd_ref[0] * HEAD_DIM ** -0.5).astype(BF16)

    def row_t(shape):
        return lax.broadcasted_iota(jnp.int32, shape, 0) % s

    def lanes_of(shape):
        return lax.broadcasted_iota(jnp.int32, shape, 1)

    def pad_rows(x):
        return jnp.concatenate([x, jnp.zeros((LANE - x.shape[0], x.shape[1]), x.dtype)], axis=0).astype(BF16)

    kct, vct = _compress_core(lambda sec, pg: pg_refs[pg][0, 0, sec * D_KV:(sec + 1) * D_KV, :],
                              npg, xs_ref, bd_ref, pe_ref)

    sc = _dot(qb, kct.astype(BF16)) + bc_ref[...]
    c_l = lanes_of((nl, nch))
    dc = p + row_t((nl, nch)) - CMP_STRIDE * c_l - (CMP_BLOCK - 1)
    pc = _softmax_lanes(sc, (dc >= 0) & (c_l < nch - 1))
    o_c = _dot_nt(pc.astype(BF16), vct.astype(BF16))

    imp = _dot_hi(gsum_ref[...], _dot_hi(pc, c2s_ref[...]))
    j_l = lanes_of((nl, LANE))
    qpos = p + row_t((nl, LANE))
    cur = qpos // SEL_BLOCK
    valid = (j_l * SEL_BLOCK <= qpos) & (j_l < nb)
    forced = (j_l == 0) | (j_l == cur) | (j_l == cur - 1)
    score = jnp.where(valid, jnp.where(forced, FORCED, imp), -jnp.inf)
    rank = jnp.zeros((nl, LANE), jnp.int32)
    for jp in range(nb):
        cj = score[:, jp:jp + 1]
        beats = (cj > score) | ((cj == score) & (j_l > jp))
        rank = rank + beats.astype(jnp.int32)
    sel = jnp.where(rank < n_sel, 1.0, 0.0).astype(BF16)
    maskf = _dot(sel, e_ref[...])

    ss = jnp.concatenate([_dot(qb, r[0, 0, 2 * D_KV:3 * D_KV, :].astype(BF16)) for r in pg_refs]
                         + [_dot_nt(qb, pad_rows(new_ref[0, 0]))], axis=1) + bs_ref[...]
    lk = p + LANE
    ds_ = p + row_t((nl, lk)) - lanes_of((nl, lk))
    ps = _softmax_lanes(ss, (maskf > 0.5) & (ds_ >= 0)).astype(BF16)
    o_s = _dot(ps[:, p:], pad_rows(new_ref[0, 1]))
    for g, r in enumerate(pg_refs):
        o_s = o_s + _dot_nt(ps[:, g * PAGE:(g + 1) * PAGE], r[0, 0, 3 * D_KV:4 * D_KV, :].astype(BF16))

    sw = jnp.concatenate([_dot(qb, cw_ref[0, 0, 0:D_KV, :].astype(BF16)),
                          _dot_nt(qb, pad_rows(new_ref[0, 2]))], axis=1) + bw_ref[...]
    lw = wb + LANE
    w_l = lanes_of((nl, lw))
    dw = wb + row_t((nl, lw)) - w_l
    pw = _softmax_lanes(sw, (dw >= 0) & (dw < WINDOW) & (p - wb + w_l >= 0)).astype(BF16)
    o_w = _dot_nt(pw[:, :wb], cw_ref[0, 0, D_KV:2 * D_KV, :].astype(BF16)) + _dot(pw[:, wb:], pad_rows(new_ref[0, 3]))

    g = jax.nn.sigmoid(ngt_ref[0])
    o_ref[0] = g[:, 0:1] * o_c + g[:, 1:2] * o_s + g[:, 2:3] * o_w


def _attn_sample(cache_t, pt_flat, layer, qbd, ngt, newkv, cwin_t, bd, pe, bc, bs, bw, c2s, gsum, emat,
                 npg, s, p, wb, nb):
    dbs = qbd.shape[0]
    nl = N_HEADS * s
    nch = npg * CPP
    pgs = [pl.BlockSpec((1, 1, 4 * D_KV, PAGE), (lambda b, pt, g=g: (pt[b * npg + g], layer, 0, 0)))
           for g in range(npg)]
    const = lambda a: pl.BlockSpec(a.shape, lambda b, pt: (0,) * a.ndim)
    grid_spec = pltpu.PrefetchScalarGridSpec(
        num_scalar_prefetch=1,
        grid=(dbs,),
        in_specs=pgs + [
            pl.BlockSpec((1, nl, D_KV), lambda b, pt: (b, 0, 0)),
            pl.BlockSpec((1, nl, LANE), lambda b, pt: (b, 0, 0)),
            pl.BlockSpec((1,) + newkv.shape[1:], lambda b, pt: (b, 0, 0, 0)),
            pl.BlockSpec((1, 1, 2 * D_KV, wb), lambda b, pt: (b, layer, 0, 0)),
            const(bd), const(pe), const(bc), const(bs), const(bw), const(c2s), const(gsum), const(emat)],
        out_specs=pl.BlockSpec((1, nl, D_KV), lambda b, pt: (b, 0, 0)),
        scratch_shapes=[pltpu.VMEM((2, nch * XS_PITCH, LANE), F32)],
    )
    return pl.pallas_call(
        functools.partial(_attn_sample_kernel, npg=npg, s=s, p=p, wb=wb, nb=nb),
        grid_spec=grid_spec,
        out_shape=jax.ShapeDtypeStruct((dbs, nl, D_KV), F32),
        compiler_params=_params(("arbitrary",)),
        name="attn_sample",
    )(pt_flat, *([cache_t] * npg), qbd, ngt, newkv, cwin_t, bd, pe, bc, bs, bw, c2s, gsum, emat)


def _merge_kernel(h_ref, ga_ref, att_ref, bz_ref, ma_ref, mb_ref, wa_ref, wb_ref, wo_ref, fg_ref, o_ref, *, final):
    y_a = _dot(ga_ref[...], wa_ref[...])
    xt = jnp.concatenate([att_ref[k] * _silu(bz_ref[k]) for k in range(att_ref.shape[0])], axis=1)
    y_b = _dot_tn(xt.astype(BF16), wb_ref[...])
    mix = jax.nn.sigmoid(ma_ref[...]) * y_a + jax.nn.sigmoid(mb_ref[...]) * y_b
    out = h_ref[...] + _dot(mix.astype(BF16), wo_ref[...])
    if final:
        ms = jnp.mean(out * out, axis=-1, keepdims=True)
        out = out * lax.rsqrt(ms + EPS) * fg_ref[...]
    o_ref[...] = out


def _merge(h, ga, att_t, bz_t, pm, wa, wb, wo, fg, tm, final):
    n = h.shape[0]
    tl = tm // LANE
    row = pl.BlockSpec((tm, 1024), lambda i: (i, 0))
    col = lambda k: pl.BlockSpec((tm, 1024), lambda i: (i, k))
    tsp = pl.BlockSpec((tl, D_ATT, LANE), lambda i: (i, 0, 0))
    wsp = pl.BlockSpec((1024, 1024), lambda i: (0, 0))
    return pl.pallas_call(
        functools.partial(_merge_kernel, final=final),
        grid=(n // tm,),
        in_specs=[row, row, tsp, tsp, col(COL_MA), col(COL_MB),
                  wsp, wsp, wsp, pl.BlockSpec((1, 1024), lambda i: (0, 0))],
        out_specs=row,
        out_shape=jax.ShapeDtypeStruct((n, D_MODEL), F32),
        compiler_params=_params(("arbitrary",)),
        name="merge",
    )(h, ga, att_t, bz_t, pm, pm, wa, wb, wo, fg)


def _cmp_to_sel(nc, nb):
    c0 = np.arange(nc)[:, None] * CMP_STRIDE
    j0 = np.arange(nb)[None, :] * SEL_BLOCK
    ov = np.minimum(c0 + CMP_BLOCK, j0 + SEL_BLOCK) - np.maximum(c0, j0)
    return np.maximum(ov, 0).astype(np.float32) / CMP_BLOCK


def _largest_divisor(n, cap, mult=1):
    for d in range(min(n, cap), 0, -1):
        if n % d == 0 and d % mult == 0:
            return d
    raise ValueError("no tile size")


def _tokens_major(xt):
    return xt.transpose(0, 2, 1).reshape(xt.shape[0] * LANE, xt.shape[1])


def kernel(x_prompt, x_sample, cache_kv, cache_win, state_conv, page_table, rel_bias, norm_g, w_in, conv_w,
           w_cmp, pe_cmp, w_a_out, w_b_out, w_o, final_g):
    bsz, t, _ = x_prompt.shape
    dbs, s, _ = x_sample.shape
    depth = w_in.shape[0]
    n_pool = cache_kv.shape[0]
    npg = page_table.shape[1]
    p = npg * PAGE
    wb = cache_win.shape[2]
    nq = t // Q_BLOCK
    nch_p = t // CMP_STRIDE
    nch_s = (p + s) // CMP_STRIDE
    nb_p = -(-t // SEL_BLOCK)
    nb_s = -(-(p + s) // SEL_BLOCK)
    ns = dbs * s
    assert t % Q_BLOCK == 0 and nq % 2 == 0 and nch_p == LANE and nb_p % 8 == 0, "prompt length not supported"
    assert 2 <= s <= 8 and wb == min(WINDOW, p) and nch_s == p // CMP_STRIDE == LANE and nb_s <= LANE
    assert ns % LANE == 0, "sample shape not supported"
    nl = N_HEADS * s
    lk, lw = p + LANE, wb + LANE

    ng_perm = np.array([_O_NG + head * 3 + br for br in range(3) for head in range(N_HEADS)])
    perm_nat = np.concatenate([np.arange(0, _O_Q), np.arange(_O_MA, _O_END)])
    perm_tr = np.concatenate([np.arange(_O_Q, _O_KC), np.arange(_O_BZ, _O_MA), np.arange(_O_KC, _O_NG), ng_perm])
    w_nat = w_in[:, :, perm_nat].astype(BF16)
    w_tr = jnp.pad(w_in.transpose(0, 2, 1)[:, perm_tr, :],
                   ((0, 0), (0, D_TR - perm_tr.size), (0, 0))).astype(BF16)
    wa_b, wb_b, wo_b = w_a_out.astype(BF16), w_b_out.astype(BF16), w_o.astype(BF16)
    eye_kv = jnp.eye(N_KV, dtype=F32)
    bd = jnp.einsum('lsjde,hk->lsjhdke', w_cmp, eye_kv).reshape(depth, 2, CMP_BLOCK, D_KV, D_KV).astype(BF16)
    pe_t = jnp.tile(pe_cmp, (1, 1, 1, N_KV))
    g2 = norm_g.reshape(depth, 1, D_MODEL)
    fg2 = final_g.reshape(1, D_MODEL)

    tabs = _bias_prompt(rel_bias, nq)
    c31 = rel_bias[N_BUCKETS - 1]
    rbx = jnp.repeat(rel_bias.T, s, axis=0)
    bc, bs, bw = _bias_sample(rbx, s, p, wb, nch_s, lk, lw)
    c2st_p = np.zeros((nb_p, LANE), np.float32)
    c2st_p[:, :nch_p - 1] = _cmp_to_sel(nch_p - 1, nb_p).T
    c2s_s = np.zeros((LANE, LANE), np.float32)
    c2s_s[:nch_s - 1, :nb_s] = _cmp_to_sel(nch_s - 1, nb_s)
    rl = np.arange(nl)
    gsum = ((rl[:, None] // (GROUP * s) == rl[None, :] // (GROUP * s))
            & (rl[:, None] % s == rl[None, :] % s)).astype(np.float32)
    emat_s = (np.arange(lk)[None, :] // SEL_BLOCK == np.arange(LANE)[:, None])
    c2st_p, c2s_s, gsum = jnp.asarray(c2st_p), jnp.asarray(c2s_s), jnp.asarray(gsum)
    emat_s = jnp.asarray(emat_s, dtype=BF16)

    cache_t = cache_kv.transpose(0, 1, 3, 4, 5, 2).reshape(n_pool, depth, 4 * D_KV, PAGE)
    cwin_t = cache_win.transpose(0, 1, 3, 4, 5, 2).reshape(dbs, depth, 2 * D_KV, wb)
    pt_flat = page_table.reshape(-1).astype(jnp.int32)

    tm_p = _largest_divisor(bsz * t, 1024, LANE)
    tt = _largest_divisor(t, 512, 8)
    tm_m = _largest_divisor(bsz * t, 512, LANE)

    hp = x_prompt.reshape(bsz * t, D_MODEL)
    hs = x_sample.reshape(ns, D_MODEL)
    kv_p, kv_s, win_p, win_s, conv_p, conv_s = [], [], [], [], [], []
    zs = jnp.zeros((dbs, s - 1, D_CONV), F32)
    for l in range(depth):
        final = l == depth - 1
        pm, qt, bzt, kvt, ngt = _inproj(hp, g2[l], w_nat[l], w_tr[l], tm_p)
        ga, tail = _conv_prompt(pm.reshape(bsz, t, D_NAT), conv_w[l], tt)
        kcv = _compress_prompt(kvt, bd[l], pe_t[l], bsz, nq)
        att_t = _attn_prompt(qt, ngt, kvt, kcv, tabs, c31, c2st_p, bsz, nq)
        hp = _merge(hp, ga.reshape(bsz * t, D_CONV), att_t, bzt, pm, wa_b[l], wb_b[l], wo_b[l], fg2, tm_m, final)
        kv_nat = _tokens_major(kvt).reshape(bsz, t, 6, N_KV, HEAD_DIM)
        kv_p.append(kv_nat[:, :, :4])
        win_p.append(kv_nat[:, t - min(WINDOW, t):, 4:])
        conv_p.append(tail[:, 8 - (CONV_W - 1):])

        pm_s, qt_s, bzt_s, kvt_s, ngt_s = _inproj(hs, g2[l], w_nat[l], w_tr[l], ns)
        st = state_conv[:, l]
        s1 = jnp.concatenate([st[:, 1:2], zs], axis=1).reshape(ns, D_CONV)
        s2 = jnp.concatenate([st, zs[:, :s - 2]], axis=1).reshape(ns, D_CONV)
        ga_s, u_s = _conv_sample(pm_s, s1, s2, conv_w[l], s)
        q_s = _tokens_major(qt_s).reshape(dbs, s, N_KV, GROUP, HEAD_DIM)
        q_r = q_s.transpose(0, 2, 3, 1, 4).reshape(dbs, N_KV, GROUP * s, HEAD_DIM)
        qbd = jnp.einsum('bkmd,kj->bkmjd', q_r, eye_kv).reshape(dbs, nl, D_KV)
        ng_s = _tokens_major(ngt_s)[:, :3 * N_HEADS].reshape(dbs, s, 3, N_HEADS).transpose(0, 3, 1, 2)
        ng_s = jnp.pad(ng_s.reshape(dbs, nl, 3), ((0, 0), (0, 0), (0, LANE - 3)))
        kv_nat_s = _tokens_major(kvt_s).reshape(dbs, s, 6, D_KV)
        newkv = jnp.pad(kv_nat_s[:, :, 2:].transpose(0, 2, 1, 3), ((0, 0), (0, 0), (0, 8 - s), (0, 0)))
        o_s = _attn_sample(cache_t, pt_flat, l, qbd, ng_s, newkv, cwin_t, bd[l], pe_t[l], bc, bs, bw,
                           c2s_s, gsum, emat_s, npg, s, p, wb, nb_s)
        o6 = o_s.reshape(dbs, N_KV, GROUP, s, N_KV, HEAD_DIM)
        att_s = jnp.stack([o6[:, k, :, :, k] for k in range(N_KV)], axis=1)
        att_s = att_s.transpose(0, 3, 1, 2, 4).reshape(ns // LANE, LANE, D_ATT).transpose(0, 2, 1)
        hs = _merge(hs, ga_s, att_s, bzt_s, pm_s, wa_b[l], wb_b[l], wo_b[l], fg2, ns, final)
        kv_s.append(kv_nat_s[:, :, :4].reshape(dbs, s, 4, N_KV, HEAD_DIM))
        new_win = kv_nat_s[:, :, 4:].reshape(dbs, s, 2, N_KV, HEAD_DIM)
        wfull = jnp.concatenate([cache_win[:, l], new_win.astype(cache_win.dtype)], axis=1)
        win_s.append(wfull[:, wfull.shape[1] - min(WINDOW, wfull.shape[1]):])
        conv_s.append(u_s.reshape(dbs, s, D_CONV)[:, s - (CONV_W - 1):])

    y_prompt = hp.reshape(bsz, t, D_MODEL)
    y_sample = hs.reshape(dbs, s, D_MODEL)
    return (y_prompt, y_sample, jnp.stack(kv_p, axis=1), jnp.stack(kv_s, axis=1), jnp.stack(win_p, axis=1),
            jnp.stack(win_s, axis=1), jnp.stack(conv_p, axis=1), jnp.stack(conv_s, axis=1))
```

```python
import functools
import math

import numpy as np
import jax
import jax.numpy as jnp
from jax import lax
from jax.experimental import pallas as pl
from jax.experimental.pallas import tpu as pltpu

F32 = jnp.float32
BF16 = jnp.bfloat16

D_MODEL = 1024
N_HEADS = 16
N_KV = 4
GROUP = N_HEADS // N_KV
HEAD_DIM = 64
D_ATT = N_HEADS * HEAD_DIM
D_KV = N_KV * HEAD_DIM
D_CONV = D_MODEL
CONV_W = 3
CMP_BLOCK = 32
CMP_STRIDE = 16
SEL_BLOCK = 64
N_SEL = 8
WINDOW = 512
Q_BLOCK = 128
FORCED = 1e4
N_BUCKETS = 32
MAX_DISTANCE = 128
EPS = 1e-6
NEG = -1e30
LOG2E = math.log2(math.e)
PAGE = 128
CPP = PAGE // CMP_STRIDE

LANE = 128
TN = 512
COL_B, COL_C, COL_H, COL_ZA, COL_MA, COL_MB = range(6)
D_NAT = 6 * 1024
D_TR = 8 * TN
NG_ROWS = 64
_O_Q, _O_KC, _O_NG, _O_BZ, _O_MA, _O_END = 4096, 5120, 6656, 6704, 7728, 9776

VMEM_LIMIT = 56 * 1024 * 1024
XS_PITCH = 24

_EXACT = N_BUCKETS // 2
_THRESH = tuple(int(math.ceil(_EXACT * (MAX_DISTANCE / _EXACT) ** (k / (N_BUCKETS - _EXACT))))
                for k in range(1, N_BUCKETS - _EXACT))


def _params(sem, vmem=VMEM_LIMIT):
    return pltpu.CompilerParams(dimension_semantics=sem, vmem_limit_bytes=vmem)


def _bucket(d):
    d = jnp.maximum(d, 0)
    big = jnp.full(d.shape, _EXACT, jnp.int32)
    for thr in _THRESH:
        big = big + (d >= thr).astype(jnp.int32)
    return jnp.where(d < _EXACT, d, big)


def _dot(a, b):
    return jnp.dot(a, b, preferred_element_type=F32)


def _dot_nt(a, b):
    return lax.dot_general(a, b, (((1,), (1,)), ((), ())), preferred_element_type=F32)


def _dot_tn(a, b):
    return lax.dot_general(a, b, (((0,), (0,)), ((), ())), preferred_element_type=F32)


def _dot_hi(a, b):
    return jnp.dot(a, b, preferred_element_type=F32, precision=lax.Precision.HIGHEST)


def _silu(z):
    return z * jax.nn.sigmoid(z)


def _inproj_kernel(x_ref, g_ref, w_ref, wt_ref, main_ref, qt_ref, bzt_ref, kvt_ref, ngt_ref, xn_ref, *, n_nat):
    j = pl.program_id(1)
    ntile = qt_ref.shape[0]

    @pl.when(j == 0)
    def _():
        x = x_ref[...]
        ms = jnp.mean(x * x, axis=-1, keepdims=True)
        xn_ref[...] = (x * lax.rsqrt(ms + EPS) * g_ref[...]).astype(BF16)

    @pl.when(j < n_nat)
    def _():
        main_ref[...] = _dot(xn_ref[...], w_ref[...])

    @pl.when(j >= n_nat)
    def _():
        yt = _dot_nt(wt_ref[...], xn_ref[...])

        def put(ref, rows):
            for k in range(ntile):
                ref[k] = yt[:rows, k * LANE:(k + 1) * LANE]

        jt = j - n_nat

        @pl.when(jt < 2)
        def _():
            put(qt_ref, TN)

        @pl.when((jt >= 2) & (jt < 4))
        def _():
            put(bzt_ref, TN)

        @pl.when((jt >= 4) & (jt < 7))
        def _():
            put(kvt_ref, TN)

        @pl.when(jt == 7)
        def _():
            put(ngt_ref, NG_ROWS)


def _inproj(x, g, w_nat, w_tr, tm):
    n = x.shape[0]
    n_nat = D_NAT // TN
    nt = n // LANE
    tl = tm // LANE
    clampi = lambda v, lo, hi: jnp.minimum(jnp.maximum(v, lo), hi)
    return pl.pallas_call(
        functools.partial(_inproj_kernel, n_nat=n_nat),
        grid=(n // tm, n_nat + D_TR // TN),
        in_specs=[pl.BlockSpec((tm, D_MODEL), lambda i, j: (i, 0)),
                  pl.BlockSpec((1, D_MODEL), lambda i, j: (0, 0)),
                  pl.BlockSpec((D_MODEL, TN), lambda i, j: (0, jnp.minimum(j, n_nat - 1))),
                  pl.BlockSpec((TN, D_MODEL), lambda i, j: (jnp.maximum(j - n_nat, 0), 0))],
        out_specs=[pl.BlockSpec((tm, TN), lambda i, j: (i, jnp.minimum(j, n_nat - 1))),
                   pl.BlockSpec((tl, TN, LANE), lambda i, j: (i, clampi(j - n_nat, 0, 1), 0)),
                   pl.BlockSpec((tl, TN, LANE), lambda i, j: (i, clampi(j - n_nat - 2, 0, 1), 0)),
                   pl.BlockSpec((tl, TN, LANE), lambda i, j: (i, clampi(j - n_nat - 4, 0, 2), 0)),
                   pl.BlockSpec((tl, NG_ROWS, LANE), lambda i, j: (i, 0, 0))],
        out_shape=[jax.ShapeDtypeStruct((n, D_NAT), F32),
                   jax.ShapeDtypeStruct((nt, D_ATT, LANE), F32),
                   jax.ShapeDtypeStruct((nt, D_ATT, LANE), F32),
                   jax.ShapeDtypeStruct((nt, 6 * D_KV, LANE), F32),
                   jax.ShapeDtypeStruct((nt, NG_ROWS, LANE), F32)],
        scratch_shapes=[pltpu.VMEM((tm, D_MODEL), BF16)],
        compiler_params=_params(("arbitrary", "arbitrary")),
        name="inproj",
    )(x, g, w_nat, w_tr)


def _conv_prompt_kernel(b_ref, c_ref, h_ref, z_ref, cp_ref, hp_ref, w_ref, ga_ref, tail_ref):
    t = pl.program_id(1)
    u = c_ref[0] * h_ref[0]
    up = jnp.where(t > 0, cp_ref[0] * hp_ref[0], 0.0)
    tt = u.shape[0]
    row = lax.broadcasted_iota(jnp.int32, u.shape, 0)
    um1 = jnp.where(row == 0, up[7:8], pltpu.roll(u, 1, 0))
    um2 = jnp.where(row == 0, up[6:7], jnp.where(row == 1, up[7:8], pltpu.roll(u, 2, 0)))
    w = w_ref[...]
    conv = w[0:1] * um2 + w[1:2] * um1 + w[2:3] * u
    ga_ref[0] = (b_ref[0] * conv * _silu(z_ref[0])).astype(BF16)

    @pl.when(t == pl.num_programs(1) - 1)
    def _():
        tail_ref[0] = u[tt - 8:]


def _conv_prompt(p3, conv_w, tt):
    bsz, t, _ = p3.shape
    hb = tt // 8
    col = lambda k: (lambda b, i: (b, i, k))
    halo = lambda k: (lambda b, i: (b, jnp.maximum(i * hb - 1, 0), k))
    return pl.pallas_call(
        _conv_prompt_kernel,
        grid=(bsz, t // tt),
        in_specs=[pl.BlockSpec((1, tt, 1024), col(COL_B)),
                  pl.BlockSpec((1, tt, 1024), col(COL_C)),
                  pl.BlockSpec((1, tt, 1024), col(COL_H)),
                  pl.BlockSpec((1, tt, 1024), col(COL_ZA)),
                  pl.BlockSpec((1, 8, 1024), halo(COL_C)),
                  pl.BlockSpec((1, 8, 1024), halo(COL_H)),
                  pl.BlockSpec((CONV_W, D_CONV), lambda b, i: (0, 0))],
        out_specs=[pl.BlockSpec((1, tt, 1024), lambda b, i: (b, i, 0)),
                   pl.BlockSpec((1, 8, 1024), lambda b, i: (b, 0, 0))],
        out_shape=[jax.ShapeDtypeStruct((bsz, t, D_CONV), BF16),
                   jax.ShapeDtypeStruct((bsz, 8, D_CONV), F32)],
        compiler_params=_params(("arbitrary", "arbitrary")),
        name="conv_prompt",
    )(p3, p3, p3, p3, p3, p3, conv_w)


def _conv_sample_kernel(b_ref, c_ref, h_ref, z_ref, s1_ref, s2_ref, w_ref, ga_ref, u_ref, *, s):
    u = c_ref[...] * h_ref[...]
    t = lax.broadcasted_iota(jnp.int32, u.shape, 0) % s
    um1 = jnp.where(t >= 1, pltpu.roll(u, 1, 0), s1_ref[...])
    um2 = jnp.where(t >= 2, pltpu.roll(u, 2, 0), s2_ref[...])
    w = w_ref[...]
    conv = w[0:1] * um2 + w[1:2] * um1 + w[2:3] * u
    ga_ref[...] = (b_ref[...] * conv * _silu(z_ref[...])).astype(BF16)
    u_ref[...] = u


def _conv_sample(p2, s1, s2, conv_w, s):
    n = p2.shape[0]
    col = lambda k: pl.BlockSpec((n, 1024), lambda i: (0, k))
    full = pl.BlockSpec((n, 1024), lambda i: (0, 0))
    return pl.pallas_call(
        functools.partial(_conv_sample_kernel, s=s),
        grid=(1,),
        in_specs=[col(COL_B), col(COL_C), col(COL_H), col(COL_ZA), full, full,
                  pl.BlockSpec((CONV_W, D_CONV), lambda i: (0, 0))],
        out_specs=[full, full],
        out_shape=[jax.ShapeDtypeStruct((n, D_CONV), BF16),
                   jax.ShapeDtypeStruct((n, D_CONV), F32)],
        compiler_params=_params(("arbitrary",)),
        name="conv_sample",
    )(p2, p2, p2, p2, s1, s2, conv_w)


def _compress_core(get_page, npg, xs_ref, bd_ref, pe_ref):
    nch = npg * CPP
    outs = []
    for sec in range(2):
        for pg in range(npg):
            x = get_page(sec, pg).T
            for half in range(2):
                for c in range(CPP):
                    r0 = (pg * CPP + c) * XS_PITCH
                    xs_ref[half, r0:r0 + CMP_STRIDE, :] = x[c * CMP_STRIDE:(c + 1) * CMP_STRIDE,
                                                            half * LANE:(half + 1) * LANE]
        lo = jnp.zeros((nch, D_KV), F32)
        hi = jnp.zeros((nch, D_KV), F32)
        for j in range(CMP_STRIDE):
            xj = jnp.concatenate([xs_ref[half, pl.ds(j, nch, stride=XS_PITCH), :] for half in range(2)], axis=1)
            lo = lo + _dot((xj + pe_ref[sec, j:j + 1, :]).astype(BF16), bd_ref[sec, j])
            hi = hi + _dot((xj + pe_ref[sec, CMP_STRIDE + j:CMP_STRIDE + j + 1, :]).astype(BF16),
                           bd_ref[sec, CMP_STRIDE + j])
        out = lo + pltpu.roll(hi, nch - 1, 0)
        row = lax.broadcasted_iota(jnp.int32, out.shape, 0)
        out = jnp.where(row == nch - 1, 0.0, out)
        outs.append(out.T)
    return outs


def _compress_prompt_kernel(kv_ref, bd_ref, pe_ref, o_ref, xs_ref):
    npg = kv_ref.shape[0]
    kct, vct = _compress_core(lambda sec, pg: kv_ref[pg, sec * D_KV:(sec + 1) * D_KV, :], npg, xs_ref, bd_ref, pe_ref)
    o_ref[0, 0:D_KV, :] = kct
    o_ref[0, D_KV:2 * D_KV, :] = vct


def _compress_prompt(kvt, bd, pe, bsz, nq):
    nch = nq * CPP
    return pl.pallas_call(
        _compress_prompt_kernel,
        grid=(bsz,),
        in_specs=[pl.BlockSpec((nq, 2 * D_KV, LANE), lambda b: (b, 0, 0)),
                  pl.BlockSpec(bd.shape, lambda b: (0, 0, 0, 0)),
                  pl.BlockSpec(pe.shape, lambda b: (0, 0, 0))],
        out_specs=pl.BlockSpec((1, 2 * D_KV, nch), lambda b: (b, 0, 0)),
        out_shape=jax.ShapeDtypeStruct((bsz, 2 * D_KV, nch), F32),
        scratch_shapes=[pltpu.VMEM((2, nch * XS_PITCH, LANE), F32)],
        compiler_params=_params(("arbitrary",)),
        name="compress_prompt",
    )(kvt, bd, pe)


def _bias_prompt_kernel(rb_ref, o_ref, *, nq):
    s = pl.program_id(0)
    r = lax.broadcasted_iota(jnp.int32, (Q_BLOCK, LANE), 0)
    c = lax.broadcasted_iota(jnp.int32, (Q_BLOCK, LANE), 1)
    d_cmp = s * Q_BLOCK + c - CMP_STRIDE * r - (CMP_BLOCK - 1)
    d = jnp.where(s < nq, d_cmp, jnp.where(s == nq, c - r, jnp.where(s == nq + 1, Q_BLOCK + c - r, WINDOW + c - r)))
    hidden = jnp.where(s == nq, (c < r).astype(jnp.int32), jnp.where(s == nq + 2, (r <= c).astype(jnp.int32), 0))
    visible = hidden == 0
    bk = _bucket(d)
    for head in range(N_HEADS):
        acc = jnp.zeros((Q_BLOCK, LANE), F32)
        for b in range(N_BUCKETS):
            acc = jnp.where(bk == b, rb_ref[b, head] * LOG2E, acc)
        g = head % GROUP
        o_ref[0, head // GROUP, :, g * Q_BLOCK:(g + 1) * Q_BLOCK] = jnp.where(visible, acc, NEG)


def _bias_prompt(rel_bias, nq):
    return pl.pallas_call(
        functools.partial(_bias_prompt_kernel, nq=nq),
        grid=(nq + 3,),
        in_specs=[pl.BlockSpec(memory_space=pltpu.SMEM)],
        out_specs=pl.BlockSpec((1, N_KV, Q_BLOCK, GROUP * Q_BLOCK), lambda s: (s, 0, 0, 0)),
        out_shape=jax.ShapeDtypeStruct((nq + 3, N_KV, Q_BLOCK, GROUP * Q_BLOCK), F32),
        compiler_params=_params(("arbitrary",)),
        name="bias_prompt",
    )(rel_bias)


def _bias_sample_kernel(rbx_ref, bc_ref, bs_ref, bw_ref, *, s, p, wb):
    def table(o_ref, dfn):
        shape = o_ref.shape
        t = lax.broadcasted_iota(jnp.int32, shape, 0) % s
        k = lax.broadcasted_iota(jnp.int32, shape, 1)
        bk = _bucket(dfn(t, k))
        acc = jnp.zeros(shape, F32)
        for b in range(N_BUCKETS):
            acc = jnp.where(bk == b, rbx_ref[:, b:b + 1], acc)
        o_ref[...] = acc

    table(bc_ref, lambda t, k: p + t - CMP_STRIDE * k - (CMP_BLOCK - 1))
    table(bs_ref, lambda t, k: p + t - k)
    table(bw_ref, lambda t, k: wb + t - k)


def _bias_sample(rbx, s, p, wb, nch, lk, lw):
    nl = N_HEADS * s
    return pl.pallas_call(
        functools.partial(_bias_sample_kernel, s=s, p=p, wb=wb),
        out_shape=[jax.ShapeDtypeStruct((nl, nch), F32),
                   jax.ShapeDtypeStruct((nl, lk), F32),
                   jax.ShapeDtypeStruct((nl, lw), F32)],
        compiler_params=pltpu.CompilerParams(vmem_limit_bytes=VMEM_LIMIT),
        name="bias_sample",
    )(rbx)


def _attn_prompt_kernel(c31_ref, qt_ref, ngt_ref, kcv_ref, ks_ref, vs_ref, kw_ref, vw_ref, tc_ref, t0_ref, t1_ref, t4_ref,
                        c2st_ref, o_ref, q_scr, selb_ref, ocmp_ref, osel_ref, *, nc, nb):
    i = pl.program_id(1)
    width = GROUP * Q_BLOCK
    key = lax.broadcasted_iota(jnp.int32, (Q_BLOCK, width), 0)
    lq = lax.broadcasted_iota(jnp.int32, (Q_BLOCK, width), 1) % Q_BLOCK
    dc = i * Q_BLOCK + lq - CMP_STRIDE * key - (CMP_BLOCK - 1)
    mask_c = (dc >= 0) & (key < nc)

    nbp = selb_ref.shape[1]
    jj = lax.broadcasted_iota(jnp.int32, (nbp, Q_BLOCK), 0)
    qp = i * Q_BLOCK + lax.broadcasted_iota(jnp.int32, (nbp, Q_BLOCK), 1)
    cur = qp // SEL_BLOCK
    valid = (jj * SEL_BLOCK <= qp) & (jj < nb)
    forced = (jj == 0) | (jj == cur) | (jj == cur - 1)
    n_sel = min(N_SEL, nb)
    half = SEL_BLOCK
    heads = range(N_KV)
    hsl = [slice(h * HEAD_DIM, (h + 1) * HEAD_DIM) for h in heads]

    qts, ocmps, selbs = [], [], []
    for h in heads:
        qt = jnp.concatenate(
            [qt_ref[0, (GROUP * h + g) * HEAD_DIM:(GROUP * h + g + 1) * HEAD_DIM, :] for g in range(GROUP)],
            axis=1)
        qts.append((qt * (HEAD_DIM ** -0.5 * LOG2E)).astype(BF16))

    for h in heads:
        s = _dot_tn(kcv_ref[0, hsl[h], :].astype(BF16), qts[h]) + tc_ref[0, h]
        s = jnp.where(mask_c, s, NEG)
        p = jnp.where(mask_c, jnp.exp2(s - jnp.max(s, axis=0, keepdims=True)), 0.0)
        p = p / jnp.maximum(jnp.sum(p, axis=0, keepdims=True), 1e-30)
        ocmps.append(_dot(kcv_ref[0, D_KV + h * HEAD_DIM:D_KV + (h + 1) * HEAD_DIM, :].astype(BF16), p.astype(BF16)))

        psum = p[:, 0:Q_BLOCK]
        for g in range(1, GROUP):
            psum = psum + p[:, g * Q_BLOCK:(g + 1) * Q_BLOCK]
        imp = _dot_hi(c2st_ref[...], psum)
        score = jnp.where(valid, jnp.where(forced, FORCED, imp), -jnp.inf)
        rank = jnp.zeros((nbp, Q_BLOCK), jnp.int32)
        for jp in range(nb):
            rj = score[jp:jp + 1, :]
            beats = (rj > score) | ((rj == score) & (jj > jp))
            rank = rank + beats.astype(jnp.int32)
        selbs.append(jnp.concatenate([jnp.where(rank < n_sel, 0.0, NEG)] * GROUP, axis=1))

    for h in heads:
        q_scr[h] = qts[h]
        ocmp_ref[h] = ocmps[h]
        selb_ref[h] = selbs[h]

    gl = lax.broadcasted_iota(jnp.int32, (1, width), 1) // Q_BLOCK
    cbrow = []
    for h in heads:
        r = jnp.zeros((1, width), F32)
        for g in range(GROUP):
            r = jnp.where(gl == g, c31_ref[GROUP * h + g] * LOG2E, r)
        cbrow.append(r)

    init = tuple((jnp.full((1, width), NEG, F32), jnp.zeros((1, width), F32), jnp.zeros((HEAD_DIM, width), F32))
                 for _ in heads)

    def update(st, sc, v):
        m_old, l_old, acc = st
        m_new = jnp.maximum(m_old, jnp.max(sc, axis=0, keepdims=True))
        pr = jnp.exp2(sc - m_new)
        alpha = jnp.exp2(m_old - m_new)
        return (m_new, alpha * l_old + jnp.sum(pr, axis=0, keepdims=True), alpha * acc + _dot(v, pr.astype(BF16)))

    def tiles(state, k_ref, v_ref, kt, bias_of):
        scs = [bias_of(h, _dot_tn(k_ref[kt, hsl[h], :].astype(BF16), q_scr[h])) for h in heads]
        return tuple(update(state[h], scs[h], v_ref[kt, hsl[h], :].astype(BF16)) for h in heads)

    def with_table(t_ref, sel_kt=None):
        def f(h, sc):
            sc = sc + t_ref[0, h]
            if sel_kt is None:
                return sc
            r0 = selb_ref[h, pl.ds(2 * sel_kt, 1), :]
            r1 = selb_ref[h, pl.ds(2 * sel_kt + 1, 1), :]
            return jnp.concatenate([sc[:half] + r0, sc[half:] + r1], axis=0)
        return f

    def far_sel(kt):
        def f(h, sc):
            r0 = selb_ref[h, pl.ds(2 * kt, 1), :] + cbrow[h]
            r1 = selb_ref[h, pl.ds(2 * kt + 1, 1), :] + cbrow[h]
            return jnp.concatenate([sc[:half] + r0, sc[half:] + r1], axis=0)
        return f

    def finish(st):
        return st[2] / jnp.maximum(st[1], 1e-30)

    def when(pred, fn, state):
        return lax.cond(pred, fn, lambda st: st, state)

    st = lax.fori_loop(0, jnp.maximum(i - 1, 0),
                       lambda kt, st: tiles(st, ks_ref, vs_ref, kt, far_sel(kt)), init)
    st = when(i >= 1, lambda st: tiles(st, ks_ref, vs_ref, i - 1, with_table(t1_ref, i - 1)), st)
    st = tiles(st, ks_ref, vs_ref, i, with_table(t0_ref, i))
    for h in heads:
        osel_ref[h] = finish(st[h])

    st = tiles(init, kw_ref, vw_ref, i, with_table(t0_ref))
    st = when(i >= 1, lambda st: tiles(st, kw_ref, vw_ref, i - 1, with_table(t1_ref)), st)
    for off in (2, 3):
        st = when(i >= off, lambda st, off=off: tiles(st, kw_ref, vw_ref, i - off, lambda h, sc: sc + cbrow[h]), st)
    st = when(i >= 4, lambda st: tiles(st, kw_ref, vw_ref, i - 4, with_table(t4_ref)), st)

    sig = jax.nn.sigmoid(ngt_ref[0])
    for h in heads:
        o_cmp, o_sel, o_win = ocmp_ref[h], osel_ref[h], finish(st[h])
        for g in range(GROUP):
            head = GROUP * h + g
            ls = slice(g * Q_BLOCK, (g + 1) * Q_BLOCK)
            o = (sig[head:head + 1, :] * o_cmp[:, ls] + sig[N_HEADS + head:N_HEADS + head + 1, :] * o_sel[:, ls]
                 + sig[2 * N_HEADS + head:2 * N_HEADS + head + 1, :] * o_win[:, ls])
            o_ref[0, head * HEAD_DIM:(head + 1) * HEAD_DIM, :] = o


def _attn_prompt(qt, ngt, kvt, kcv, tabs, c31, c2st, bsz, nq):
    nch = kcv.shape[2]
    nb = c2st.shape[0]
    width = GROUP * Q_BLOCK
    kvspec = lambda k: pl.BlockSpec((nq, D_KV, LANE), lambda b, i: (b, k, 0))
    tab = lambda k: pl.BlockSpec((1, N_KV, Q_BLOCK, width), lambda b, i: (nq + k, 0, 0, 0))
    return pl.pallas_call(
        functools.partial(_attn_prompt_kernel, nc=nch - 1, nb=nb),
        grid=(bsz, nq),
        in_specs=[pl.BlockSpec(memory_space=pltpu.SMEM),
                  pl.BlockSpec((1, D_ATT, LANE), lambda b, i: (b * nq + i, 0, 0)),
                  pl.BlockSpec((1, NG_ROWS, LANE), lambda b, i: (b * nq + i, 0, 0)),
                  pl.BlockSpec((1, 2 * D_KV, nch), lambda b, i: (b, 0, 0)),
                  kvspec(2), kvspec(3), kvspec(4), kvspec(5),
                  pl.BlockSpec((1, N_KV, Q_BLOCK, width), lambda b, i: (i, 0, 0, 0)),
                  tab(0), tab(1), tab(2),
                  pl.BlockSpec(c2st.shape, lambda b, i: (0, 0))],
        out_specs=pl.BlockSpec((1, D_ATT, LANE), lambda b, i: (b * nq + i, 0, 0)),
        out_shape=jax.ShapeDtypeStruct((bsz * nq, D_ATT, LANE), F32),
        scratch_shapes=[pltpu.VMEM((N_KV, HEAD_DIM, width), BF16),
                        pltpu.VMEM((N_KV, nb, width), F32),
                        pltpu.VMEM((N_KV, HEAD_DIM, width), F32),
                        pltpu.VMEM((N_KV, HEAD_DIM, width), F32)],
        compiler_params=_params(("arbitrary", "arbitrary")),
        name="attn_prompt",
    )(c31, qt, ngt, kcv, kvt, kvt, kvt, kvt, tabs, tabs, tabs, tabs, c2st)


def _softmax_lanes(s, mask):
    s = jnp.where(mask, s, NEG)
    p = jnp.where(mask, jnp.exp(s - jnp.max(s, axis=-1, keepdims=True)), 0.0)
    return p / jnp.maximum(jnp.sum(p, axis=-1, keepdims=True), 1e-30)


def _attn_sample_kernel(pt_ref, *refs, npg, s, p, wb, nb):
    del pt_ref
    pg_refs = refs[:npg]
    (qbd_ref, ngt_ref, new_ref, cw_ref, bd_ref, pe_ref, bc_ref, bs_ref, bw_ref,
     c2s_ref, gsum_ref, e_ref, o_ref, xs_ref) = refs[npg:]
    nl = N_HEADS * s
    nch = npg * CPP
    n_sel = min(N_SEL, nb)
    qb = (qbd_ref[0] * HEAD_DIM ** -0.5).astype(BF16)

    def row_t(shape):
        return lax.broadcasted_iota(jnp.int32, shape, 0) % s

    def lanes_of(shape):
        return lax.broadcasted_iota(jnp.int32, shape, 1)

    def pad_rows(x):
        return jnp.concatenate([x, jnp.zeros((LANE - x.shape[0], x.shape[1]), x.dtype)], axis=0).astype(BF16)

    kct, vct = _compress_core(lambda sec, pg: pg_refs[pg][0, 0, sec * D_KV:(sec + 1) * D_KV, :],
                              npg, xs_ref, bd_ref, pe_ref)

    sc = _dot(qb, kct.astype(BF16)) + bc_ref[...]
    c_l = lanes_of((nl, nch))
    dc = p + row_t((nl, nch)) - CMP_STRIDE * c_l - (CMP_BLOCK - 1)
    pc = _softmax_lanes(sc, (dc >= 0) & (c_l < nch - 1))
    o_c = _dot_nt(pc.astype(BF16), vct.astype(BF16))

    imp = _dot_hi(gsum_ref[...], _dot_hi(pc, c2s_ref[...]))
    j_l = lanes_of((nl, LANE))
    qpos = p + row_t((nl, LANE))
    cur = qpos // SEL_BLOCK
    valid = (j_l * SEL_BLOCK <= qpos) & (j_l < nb)
    forced = (j_l == 0) | (j_l == cur) | (j_l == cur - 1)
    score = jnp.where(valid, jnp.where(forced, FORCED, imp), -jnp.inf)
    rank = jnp.zeros((nl, LANE), jnp.int32)
    for jp in range(nb):
        cj = score[:, jp:jp + 1]
        beats = (cj > score) | ((cj == score) & (j_l > jp))
        rank = rank + beats.astype(jnp.int32)
    sel = jnp.where(rank < n_sel, 1.0, 0.0).astype(BF16)
    maskf = _dot(sel, e_ref[...])

    ss = jnp.concatenate([_dot(qb, r[0, 0, 2 * D_KV:3 * D_KV, :].astype(BF16)) for r in pg_refs]
                         + [_dot_nt(qb, pad_rows(new_ref[0, 0]))], axis=1) + bs_ref[...]
    lk = p + LANE
    ds_ = p + row_t((nl, lk)) - lanes_of((nl, lk))
    ps = _softmax_lanes(ss, (maskf > 0.5) & (ds_ >= 0)).astype(BF16)
    o_s = _dot(ps[:, p:], pad_rows(new_ref[0, 1]))
    for g, r in enumerate(pg_refs):
        o_s = o_s + _dot_nt(ps[:, g * PAGE:(g + 1) * PAGE], r[0, 0, 3 * D_KV:4 * D_KV, :].astype(BF16))

    sw = jnp.concatenate([_dot(qb, cw_ref[0, 0, 0:D_KV, :].astype(BF16)),
                          _dot_nt(qb, pad_rows(new_ref[0, 2]))], axis=1) + bw_ref[...]
    lw = wb + LANE
    w_l = lanes_of((nl, lw))
    dw = wb + row_t((nl, lw)) - w_l
    pw = _softmax_lanes(sw, (dw >= 0) & (dw < WINDOW) & (p - wb + w_l >= 0)).astype(BF16)
    o_w = _dot_nt(pw[:, :wb], cw_ref[0, 0, D_KV:2 * D_KV, :].astype(BF16)) + _dot(pw[:, wb:], pad_rows(new_ref[0, 3]))

    g = jax.nn.sigmoid(ngt_ref[0])
    o_ref[0] = g[:, 0:1] * o_c + g[:, 1:2] * o_s + g[:, 2:3] * o_w


def _attn_sample(cache_t, pt_flat, layer, qbd, ngt, newkv, cwin_t, bd, pe, bc, bs, bw, c2s, gsum, emat,
                 npg, s, p, wb, nb):
    dbs = qbd.shape[0]
    nl = N_HEADS * s
    nch = npg * CPP
    pgs = [pl.BlockSpec((1, 1, 4 * D_KV, PAGE), (lambda b, pt, g=g: (pt[b * npg + g], layer, 0, 0)))
           for g in range(npg)]
    const = lambda a: pl.BlockSpec(a.shape, lambda b, pt: (0,) * a.ndim)
    grid_spec = pltpu.PrefetchScalarGridSpec(
        num_scalar_prefetch=1,
        grid=(dbs,),
        in_specs=pgs + [
            pl.BlockSpec((1, nl, D_KV), lambda b, pt: (b, 0, 0)),
            pl.BlockSpec((1, nl, LANE), lambda b, pt: (b, 0, 0)),
            pl.BlockSpec((1,) + newkv.shape[1:], lambda b, pt: (b, 0, 0, 0)),
            pl.BlockSpec((1, 1, 2 * D_KV, wb), lambda b, pt: (b, layer, 0, 0)),
            const(bd), const(pe), const(bc), const(bs), const(bw), const(c2s), const(gsum), const(emat)],
        out_specs=pl.BlockSpec((1, nl, D_KV), lambda b, pt: (b, 0, 0)),
        scratch_shapes=[pltpu.VMEM((2, nch * XS_PITCH, LANE), F32)],
    )
    return pl.pallas_call(
        functools.partial(_attn_sample_kernel, npg=npg, s=s, p=p, wb=wb, nb=nb),
        grid_spec=grid_spec,
        out_shape=jax.ShapeDtypeStruct((dbs, nl, D_KV), F32),
        compiler_params=_params(("arbitrary",)),
        name="attn_sample",
    )(pt_flat, *([cache_t] * npg), qbd, ngt, newkv, cwin_t, bd, pe, bc, bs, bw, c2s, gsum, emat)


def _merge_kernel(h_ref, ga_ref, att_ref, bz_ref, ma_ref, mb_ref, wa_ref, wb_ref, wo_ref, fg_ref, o_ref, *, final):
    y_a = _dot(ga_ref[...], wa_ref[...])
    xt = jnp.concatenate([att_ref[k] * _silu(bz_ref[k]) for k in range(att_ref.shape[0])], axis=1)
    y_b = _dot_tn(xt.astype(BF16), wb_ref[...])
    mix = jax.nn.sigmoid(ma_ref[...]) * y_a + jax.nn.sigmoid(mb_ref[...]) * y_b
    out = h_ref[...] + _dot(mix.astype(BF16), wo_ref[...])
    if final:
        ms = jnp.mean(out * out, axis=-1, keepdims=True)
        out = out * lax.rsqrt(ms + EPS) * fg_ref[...]
    o_ref[...] = out


def _merge(h, ga, att_t, bz_t, pm, wa, wb, wo, fg, tm, final):
    n = h.shape[0]
    tl = tm // LANE
    row = pl.BlockSpec((tm, 1024), lambda i: (i, 0))
    col = lambda k: pl.BlockSpec((tm, 1024), lambda i: (i, k))
    tsp = pl.BlockSpec((tl, D_ATT, LANE), lambda i: (i, 0, 0))
    wsp = pl.BlockSpec((1024, 1024), lambda i: (0, 0))
    return pl.pallas_call(
        functools.partial(_merge_kernel, final=final),
        grid=(n // tm,),
        in_specs=[row, row, tsp, tsp, col(COL_MA), col(COL_MB),
                  wsp, wsp, wsp, pl.BlockSpec((1, 1024), lambda i: (0, 0))],
        out_specs=row,
        out_shape=jax.ShapeDtypeStruct((n, D_MODEL), F32),
        compiler_params=_params(("arbitrary",)),
        name="merge",
    )(h, ga, att_t, bz_t, pm, pm, wa, wb, wo, fg)


def _cmp_to_sel(nc, nb):
    c0 = np.arange(nc)[:, None] * CMP_STRIDE
    j0 = np.arange(nb)[None, :] * SEL_BLOCK
    ov = np.minimum(c0 + CMP_BLOCK, j0 + SEL_BLOCK) - np.maximum(c0, j0)
    return np.maximum(ov, 0).astype(np.float32) / CMP_BLOCK


def _largest_divisor(n, cap, mult=1):
    for d in range(min(n, cap), 0, -1):
        if n % d == 0 and d % mult == 0:
            return d
    raise ValueError("no tile size")


def _tokens_major(xt):
    return xt.transpose(0, 2, 1).reshape(xt.shape[0] * LANE, xt.shape[1])


def kernel(x_prompt, x_sample, cache_kv, cache_win, state_conv, page_table, rel_bias, norm_g, w_in, conv_w,
           w_cmp, pe_cmp, w_a_out, w_b_out, w_o, final_g):
    bsz, t, _ = x_prompt.shape
    dbs, s, _ = x_sample.shape
    depth = w_in.shape[0]
    n_pool = cache_kv.shape[0]
    npg = page_table.shape[1]
    p = npg * PAGE
    wb = cache_win.shape[2]
    nq = t // Q_BLOCK
    nch_p = t // CMP_STRIDE
    nch_s = (p + s) // CMP_STRIDE
    nb_p = -(-t // SEL_BLOCK)
    nb_s = -(-(p + s) // SEL_BLOCK)
    ns = dbs * s
    assert t % Q_BLOCK == 0 and nq % 2 == 0 and nch_p == LANE and nb_p % 8 == 0, "prompt length not supported"
    assert 2 <= s <= 8 and wb == min(WINDOW, p) and nch_s == p // CMP_STRIDE == LANE and nb_s <= LANE
    assert ns % LANE == 0, "sample shape not supported"
    assert WINDOW == 4 * Q_BLOCK and SEL_BLOCK * 2 == Q_BLOCK
    nl = N_HEADS * s
    lk, lw = p + LANE, wb + LANE

    ng_perm = np.array([_O_NG + head * 3 + br for br in range(3) for head in range(N_HEADS)])
    perm_nat = np.concatenate([np.arange(0, _O_Q), np.arange(_O_MA, _O_END)])
    perm_tr = np.concatenate([np.arange(_O_Q, _O_KC), np.arange(_O_BZ, _O_MA), np.arange(_O_KC, _O_NG), ng_perm])
    w_nat = w_in[:, :, perm_nat].astype(BF16)
    w_tr = jnp.pad(w_in.transpose(0, 2, 1)[:, perm_tr, :],
                   ((0, 0), (0, D_TR - perm_tr.size), (0, 0))).astype(BF16)
    wa_b, wb_b, wo_b = w_a_out.astype(BF16), w_b_out.astype(BF16), w_o.astype(BF16)
    eye_kv = jnp.eye(N_KV, dtype=F32)
    bd = jnp.einsum('lsjde,hk->lsjhdke', w_cmp, eye_kv).reshape(depth, 2, CMP_BLOCK, D_KV, D_KV).astype(BF16)
    pe_t = jnp.tile(pe_cmp, (1, 1, 1, N_KV))
    g2 = norm_g.reshape(depth, 1, D_MODEL)
    fg2 = final_g.reshape(1, D_MODEL)

    tabs = _bias_prompt(rel_bias, nq)
    c31 = rel_bias[N_BUCKETS - 1]
    rbx = jnp.repeat(rel_bias.T, s, axis=0)
    bc, bs, bw = _bias_sample(rbx, s, p, wb, nch_s, lk, lw)
    c2st_p = np.zeros((nb_p, LANE), np.float32)
    c2st_p[:, :nch_p - 1] = _cmp_to_sel(nch_p - 1, nb_p).T
    c2s_s = np.zeros((LANE, LANE), np.float32)
    c2s_s[:nch_s - 1, :nb_s] = _cmp_to_sel(nch_s - 1, nb_s)
    rl = np.arange(nl)
    gsum = ((rl[:, None] // (GROUP * s) == rl[None, :] // (GROUP * s))
            & (rl[:, None] % s == rl[None, :] % s)).astype(np.float32)
    emat_s = (np.arange(lk)[None, :] // SEL_BLOCK == np.arange(LANE)[:, None])
    c2st_p, c2s_s, gsum = jnp.asarray(c2st_p), jnp.asarray(c2s_s), jnp.asarray(gsum)
    emat_s = jnp.asarray(emat_s, dtype=BF16)

    cache_t = cache_kv.transpose(0, 1, 3, 4, 5, 2).reshape(n_pool, depth, 4 * D_KV, PAGE)
    cwin_t = cache_win.transpose(0, 1, 3, 4, 5, 2).reshape(dbs, depth, 2 * D_KV, wb)
    pt_flat = page_table.reshape(-1).astype(jnp.int32)

    tm_p = _largest_divisor(bsz * t, 1024, LANE)
    tt = _largest_divisor(t, 512, 8)
    tm_m = _largest_divisor(bsz * t, 512, LANE)

    hp = x_prompt.reshape(bsz * t, D_MODEL)
    hs = x_sample.reshape(ns, D_MODEL)
    kv_p, kv_s, win_p, win_s, conv_p, conv_s = [], [], [], [], [], []
    zs = jnp.zeros((dbs, s - 1, D_CONV), F32)
    for l in range(depth):
        final = l == depth - 1
        pm, qt, bzt, kvt, ngt = _inproj(hp, g2[l], w_nat[l], w_tr[l], tm_p)
        ga, tail = _conv_prompt(pm.reshape(bsz, t, D_NAT), conv_w[l], tt)
        kcv = _compress_prompt(kvt, bd[l], pe_t[l], bsz, nq)
        att_t = _attn_prompt(qt, ngt, kvt, kcv, tabs, c31, c2st_p, bsz, nq)
        hp = _merge(hp, ga.reshape(bsz * t, D_CONV), att_t, bzt, pm, wa_b[l], wb_b[l], wo_b[l], fg2, tm_m, final)
        kv_nat = _tokens_major(kvt).reshape(bsz, t, 6, N_KV, HEAD_DIM)
        kv_p.append(kv_nat[:, :, :4])
        win_p.append(kv_nat[:, t - min(WINDOW, t):, 4:])
        conv_p.append(tail[:, 8 - (CONV_W - 1):])

        pm_s, qt_s, bzt_s, kvt_s, ngt_s = _inproj(hs, g2[l], w_nat[l], w_tr[l], ns)
        st = state_conv[:, l]
        s1 = jnp.concatenate([st[:, 1:2], zs], axis=1).reshape(ns, D_CONV)
        s2 = jnp.concatenate([st, zs[:, :s - 2]], axis=1).reshape(ns, D_CONV)
        ga_s, u_s = _conv_sample(pm_s, s1, s2, conv_w[l], s)
        q_s = _tokens_major(qt_s).reshape(dbs, s, N_KV, GROUP, HEAD_DIM)
        q_r = q_s.transpose(0, 2, 3, 1, 4).reshape(dbs, N_KV, GROUP * s, HEAD_DIM)
        qbd = jnp.einsum('bkmd,kj->bkmjd', q_r, eye_kv).reshape(dbs, nl, D_KV)
        ng_s = _tokens_major(ngt_s)[:, :3 * N_HEADS].reshape(dbs, s, 3, N_HEADS).transpose(0, 3, 1, 2)
        ng_s = jnp.pad(ng_s.reshape(dbs, nl, 3), ((0, 0), (0, 0), (0, LANE - 3)))
        kv_nat_s = _tokens_major(kvt_s).reshape(dbs, s, 6, D_KV)
        newkv = jnp.pad(kv_nat_s[:, :, 2:].transpose(0, 2, 1, 3), ((0, 0), (0, 0), (0, 8 - s), (0, 0)))
        o_s = _attn_sample(cache_t, pt_flat, l, qbd, ng_s, newkv, cwin_t, bd[l], pe_t[l], bc, bs, bw,
                           c2s_s, gsum, emat_s, npg, s, p, wb, nb_s)
        o6 = o_s.reshape(dbs, N_KV, GROUP, s, N_KV, HEAD_DIM)
        att_s = jnp.stack([o6[:, k, :, :, k] for k in range(N_KV)], axis=1)
        att_s = att_s.transpose(0, 3, 1, 2, 4).reshape(ns // LANE, LANE, D_ATT).transpose(0, 2, 1)
        hs = _merge(hs, ga_s, att_s, bzt_s, pm_s, wa_b[l], wb_b[l], wo_b[l], fg2, ns, final)
        kv_s.append(kv_nat_s[:, :, :4].reshape(dbs, s, 4, N_KV, HEAD_DIM))
        win_s.append(kv_nat_s[:, :, 4:].reshape(dbs, s, 2, N_KV, HEAD_DIM))
        conv_s.append(u_s.reshape(dbs, s, D_CONV)[:, s - (CONV_W - 1):])

    y_prompt = hp.reshape(bsz, t, D_MODEL)
    y_sample = hs.reshape(dbs, s, D_MODEL)
    keep = min(WINDOW, wb + s)
    win_sample = jnp.concatenate([cache_win[:, :, wb + s - keep:], jnp.stack(win_s, axis=1).astype(cache_win.dtype)],
                                 axis=2)
    return (y_prompt, y_sample, jnp.stack(kv_p, axis=1), jnp.stack(kv_s, axis=1), jnp.stack(win_p, axis=1),
            win_sample, jnp.stack(conv_p, axis=1), jnp.stack(conv_s, axis=1))
```

```python
import functools
import math

import numpy as np
import jax
import jax.numpy as jnp
from jax import lax
from jax.experimental import pallas as pl
from jax.experimental.pallas import tpu as pltpu

F32 = jnp.float32
BF16 = jnp.bfloat16

D_MODEL = 1024
N_HEADS = 16
N_KV = 4
GROUP = N_HEADS // N_KV
HEAD_DIM = 64
D_ATT = N_HEADS * HEAD_DIM
D_KV = N_KV * HEAD_DIM
D_CONV = D_MODEL
CONV_W = 3
CMP_BLOCK = 32
CMP_STRIDE = 16
SEL_BLOCK = 64
N_SEL = 8
WINDOW = 512
Q_BLOCK = 128
FORCED = 1e4
N_BUCKETS = 32
MAX_DISTANCE = 128
EPS = 1e-6
NEG = -1e30
LOG2E = math.log2(math.e)
PAGE = 128
CPP = PAGE // CMP_STRIDE

LANE = 128
TN = 512
COL_B, COL_C, COL_H, COL_ZA, COL_MA, COL_MB = range(6)
D_NAT = 6 * 1024
D_TR = 8 * TN
NG_ROWS = 64
_O_Q, _O_KC, _O_NG, _O_BZ, _O_MA, _O_END = 4096, 5120, 6656, 6704, 7728, 9776

VMEM_LIMIT = 56 * 1024 * 1024
XS_PITCH = 24

_EXACT = N_BUCKETS // 2
_THRESH = tuple(int(math.ceil(_EXACT * (MAX_DISTANCE / _EXACT) ** (k / (N_BUCKETS - _EXACT))))
                for k in range(1, N_BUCKETS - _EXACT))


def _params(sem, vmem=VMEM_LIMIT):
    return pltpu.CompilerParams(dimension_semantics=sem, vmem_limit_bytes=vmem)


def _bucket(d):
    d = jnp.maximum(d, 0)
    big = jnp.full(d.shape, _EXACT, jnp.int32)
    for thr in _THRESH:
        big = big + (d >= thr).astype(jnp.int32)
    return jnp.where(d < _EXACT, d, big)


def _dot(a, b):
    return jnp.dot(a, b, preferred_element_type=F32)


def _dot_nt(a, b):
    return lax.dot_general(a, b, (((1,), (1,)), ((), ())), preferred_element_type=F32)


def _dot_tn(a, b):
    return lax.dot_general(a, b, (((0,), (0,)), ((), ())), preferred_element_type=F32)


def _dot_hi(a, b):
    return jnp.dot(a, b, preferred_element_type=F32, precision=lax.Precision.HIGHEST)


def _silu(z):
    return z * jax.nn.sigmoid(z)


def _inproj_kernel(x_ref, g_ref, w_ref, wt_ref, main_ref, qt_ref, bzt_ref, kvt_ref, ngt_ref, xn_ref, *, n_nat):
    j = pl.program_id(1)
    ntile = qt_ref.shape[0]

    @pl.when(j == 0)
    def _():
        x = x_ref[...]
        ms = jnp.mean(x * x, axis=-1, keepdims=True)
        xn_ref[...] = (x * lax.rsqrt(ms + EPS) * g_ref[...]).astype(BF16)

    @pl.when(j < n_nat)
    def _():
        main_ref[...] = _dot(xn_ref[...], w_ref[...])

    @pl.when(j >= n_nat)
    def _():
        yt = _dot_nt(wt_ref[...], xn_ref[...])

        def put(ref, rows):
            for k in range(ntile):
                ref[k] = yt[:rows, k * LANE:(k + 1) * LANE]

        jt = j - n_nat

        @pl.when(jt < 2)
        def _():
            put(qt_ref, TN)

        @pl.when((jt >= 2) & (jt < 4))
        def _():
            put(bzt_ref, TN)

        @pl.when((jt >= 4) & (jt < 7))
        def _():
            put(kvt_ref, TN)

        @pl.when(jt == 7)
        def _():
            put(ngt_ref, NG_ROWS)


def _inproj(x, g, w_nat, w_tr, tm):
    n = x.shape[0]
    n_nat = D_NAT // TN
    nt = n // LANE
    tl = tm // LANE
    clampi = lambda v, lo, hi: jnp.minimum(jnp.maximum(v, lo), hi)
    return pl.pallas_call(
        functools.partial(_inproj_kernel, n_nat=n_nat),
        grid=(n // tm, n_nat + D_TR // TN),
        in_specs=[pl.BlockSpec((tm, D_MODEL), lambda i, j: (i, 0)),
                  pl.BlockSpec((1, D_MODEL), lambda i, j: (0, 0)),
                  pl.BlockSpec((D_MODEL, TN), lambda i, j: (0, jnp.minimum(j, n_nat - 1))),
                  pl.BlockSpec((TN, D_MODEL), lambda i, j: (jnp.maximum(j - n_nat, 0), 0))],
        out_specs=[pl.BlockSpec((tm, TN), lambda i, j: (i, jnp.minimum(j, n_nat - 1))),
                   pl.BlockSpec((tl, TN, LANE), lambda i, j: (i, clampi(j - n_nat, 0, 1), 0)),
                   pl.BlockSpec((tl, TN, LANE), lambda i, j: (i, clampi(j - n_nat - 2, 0, 1), 0)),
                   pl.BlockSpec((tl, TN, LANE), lambda i, j: (i, clampi(j - n_nat - 4, 0, 2), 0)),
                   pl.BlockSpec((tl, NG_ROWS, LANE), lambda i, j: (i, 0, 0))],
        out_shape=[jax.ShapeDtypeStruct((n, D_NAT), F32),
                   jax.ShapeDtypeStruct((nt, D_ATT, LANE), F32),
                   jax.ShapeDtypeStruct((nt, D_ATT, LANE), F32),
                   jax.ShapeDtypeStruct((nt, 6 * D_KV, LANE), F32),
                   jax.ShapeDtypeStruct((nt, NG_ROWS, LANE), F32)],
        scratch_shapes=[pltpu.VMEM((tm, D_MODEL), BF16)],
        compiler_params=_params(("arbitrary", "arbitrary")),
        name="inproj",
    )(x, g, w_nat, w_tr)


def _conv_prompt_kernel(b_ref, c_ref, h_ref, z_ref, cp_ref, hp_ref, w_ref, ga_ref, tail_ref):
    t = pl.program_id(1)
    u = c_ref[0] * h_ref[0]
    up = jnp.where(t > 0, cp_ref[0] * hp_ref[0], 0.0)
    tt = u.shape[0]
    row = lax.broadcasted_iota(jnp.int32, u.shape, 0)
    um1 = jnp.where(row == 0, up[7:8], pltpu.roll(u, 1, 0))
    um2 = jnp.where(row == 0, up[6:7], jnp.where(row == 1, up[7:8], pltpu.roll(u, 2, 0)))
    w = w_ref[...]
    conv = w[0:1] * um2 + w[1:2] * um1 + w[2:3] * u
    ga_ref[0] = (b_ref[0] * conv * _silu(z_ref[0])).astype(BF16)

    @pl.when(t == pl.num_programs(1) - 1)
    def _():
        tail_ref[0] = u[tt - 8:]


def _conv_prompt(p3, conv_w, tt):
    bsz, t, _ = p3.shape
    hb = tt // 8
    col = lambda k: (lambda b, i: (b, i, k))
    halo = lambda k: (lambda b, i: (b, jnp.maximum(i * hb - 1, 0), k))
    return pl.pallas_call(
        _conv_prompt_kernel,
        grid=(bsz, t // tt),
        in_specs=[pl.BlockSpec((1, tt, 1024), col(COL_B)),
                  pl.BlockSpec((1, tt, 1024), col(COL_C)),
                  pl.BlockSpec((1, tt, 1024), col(COL_H)),
                  pl.BlockSpec((1, tt, 1024), col(COL_ZA)),
                  pl.BlockSpec((1, 8, 1024), halo(COL_C)),
                  pl.BlockSpec((1, 8, 1024), halo(COL_H)),
                  pl.BlockSpec((CONV_W, D_CONV), lambda b, i: (0, 0))],
        out_specs=[pl.BlockSpec((1, tt, 1024), lambda b, i: (b, i, 0)),
                   pl.BlockSpec((1, 8, 1024), lambda b, i: (b, 0, 0))],
        out_shape=[jax.ShapeDtypeStruct((bsz, t, D_CONV), BF16),
                   jax.ShapeDtypeStruct((bsz, 8, D_CONV), F32)],
        compiler_params=_params(("arbitrary", "arbitrary")),
        name="conv_prompt",
    )(p3, p3, p3, p3, p3, p3, conv_w)


def _conv_sample_kernel(b_ref, c_ref, h_ref, z_ref, s1_ref, s2_ref, w_ref, ga_ref, u_ref, *, s):
    u = c_ref[...] * h_ref[...]
    t = lax.broadcasted_iota(jnp.int32, u.shape, 0) % s
    um1 = jnp.where(t >= 1, pltpu.roll(u, 1, 0), s1_ref[...])
    um2 = jnp.where(t >= 2, pltpu.roll(u, 2, 0), s2_ref[...])
    w = w_ref[...]
    conv = w[0:1] * um2 + w[1:2] * um1 + w[2:3] * u
    ga_ref[...] = (b_ref[...] * conv * _silu(z_ref[...])).astype(BF16)
    u_ref[...] = u


def _conv_sample(p2, s1, s2, conv_w, s):
    n = p2.shape[0]
    col = lambda k: pl.BlockSpec((n, 1024), lambda i: (0, k))
    full = pl.BlockSpec((n, 1024), lambda i: (0, 0))
    return pl.pallas_call(
        functools.partial(_conv_sample_kernel, s=s),
        grid=(1,),
        in_specs=[col(COL_B), col(COL_C), col(COL_H), col(COL_ZA), full, full,
                  pl.BlockSpec((CONV_W, D_CONV), lambda i: (0, 0))],
        out_specs=[full, full],
        out_shape=[jax.ShapeDtypeStruct((n, D_CONV), BF16),
                   jax.ShapeDtypeStruct((n, D_CONV), F32)],
        compiler_params=_params(("arbitrary",)),
        name="conv_sample",
    )(p2, p2, p2, p2, s1, s2, conv_w)


def _compress_core(get_page, npg, xs_ref, bdc_ref, pec_ref):
    nch = npg * CPP
    outs = []
    for sec in range(2):
        for pg in range(npg):
            x = get_page(sec, pg).T
            for half in range(2):
                for c in range(CPP):
                    r0 = (pg * CPP + c) * XS_PITCH
                    xs_ref[half, r0:r0 + CMP_STRIDE, :] = x[c * CMP_STRIDE:(c + 1) * CMP_STRIDE,
                                                            half * LANE:(half + 1) * LANE]
        acc = jnp.zeros((nch, 2 * D_KV), F32)
        for j in range(CMP_STRIDE):
            xj = jnp.concatenate([xs_ref[half, pl.ds(j, nch, stride=XS_PITCH), :] for half in range(2)], axis=1)
            acc = acc + _dot(xj.astype(BF16), bdc_ref[sec, j])
        acc = acc + pec_ref[sec, 0:1, :]
        out = acc[:, :D_KV] + pltpu.roll(acc[:, D_KV:], nch - 1, 0)
        row = lax.broadcasted_iota(jnp.int32, out.shape, 0)
        out = jnp.where(row == nch - 1, 0.0, out)
        outs.append(out.T)
    return outs


def _pe_const_kernel(bdc_ref, pe_ref, o_ref):
    for sec in range(2):
        acc = jnp.zeros((8, 2 * D_KV), F32)
        for j in range(CMP_STRIDE):
            w = bdc_ref[0, sec, j]
            lo = _dot(jnp.broadcast_to(pe_ref[0, sec, j:j + 1, :], (8, D_KV)).astype(BF16), w[:, :D_KV])
            hi = _dot(jnp.broadcast_to(pe_ref[0, sec, CMP_STRIDE + j:CMP_STRIDE + j + 1, :], (8, D_KV)).astype(BF16),
                      w[:, D_KV:])
            acc = acc + jnp.concatenate([lo, hi], axis=1)
        o_ref[0, sec] = acc


def _pe_const(bdc, pe):
    depth = bdc.shape[0]
    return pl.pallas_call(
        _pe_const_kernel,
        grid=(depth,),
        in_specs=[pl.BlockSpec((1,) + bdc.shape[1:], lambda l: (l, 0, 0, 0, 0)),
                  pl.BlockSpec((1,) + pe.shape[1:], lambda l: (l, 0, 0, 0))],
        out_specs=pl.BlockSpec((1, 2, 8, 2 * D_KV), lambda l: (l, 0, 0, 0)),
        out_shape=jax.ShapeDtypeStruct((depth, 2, 8, 2 * D_KV), F32),
        compiler_params=_params(("arbitrary",)),
        name="pe_const",
    )(bdc, pe)


def _compress_prompt_kernel(kv_ref, bdc_ref, pec_ref, o_ref, xs_ref):
    npg = kv_ref.shape[0]
    kct, vct = _compress_core(lambda sec, pg: kv_ref[pg, sec * D_KV:(sec + 1) * D_KV, :], npg, xs_ref, bdc_ref, pec_ref)
    o_ref[0, 0:D_KV, :] = kct
    o_ref[0, D_KV:2 * D_KV, :] = vct


def _compress_prompt(kvt, bdc, pec, bsz, nq):
    nch = nq * CPP
    return pl.pallas_call(
        _compress_prompt_kernel,
        grid=(bsz,),
        in_specs=[pl.BlockSpec((nq, 2 * D_KV, LANE), lambda b: (b, 0, 0)),
                  pl.BlockSpec(bdc.shape, lambda b: (0, 0, 0, 0)),
                  pl.BlockSpec(pec.shape, lambda b: (0, 0, 0))],
        out_specs=pl.BlockSpec((1, 2 * D_KV, nch), lambda b: (b, 0, 0)),
        out_shape=jax.ShapeDtypeStruct((bsz, 2 * D_KV, nch), F32),
        scratch_shapes=[pltpu.VMEM((2, nch * XS_PITCH, LANE), F32)],
        compiler_params=_params(("arbitrary",)),
        name="compress_prompt",
    )(kvt, bdc, pec)


def _bias_prompt_kernel(rb_ref, o_ref, *, nq):
    s = pl.program_id(0)
    r = lax.broadcasted_iota(jnp.int32, (Q_BLOCK, LANE), 0)
    c = lax.broadcasted_iota(jnp.int32, (Q_BLOCK, LANE), 1)
    d_cmp = s * Q_BLOCK + c - CMP_STRIDE * r - (CMP_BLOCK - 1)
    d = jnp.where(s < nq, d_cmp, jnp.where(s == nq, c - r, jnp.where(s == nq + 1, Q_BLOCK + c - r, WINDOW + c - r)))
    hidden = jnp.where(s == nq, (c < r).astype(jnp.int32), jnp.where(s == nq + 2, (r <= c).astype(jnp.int32), 0))
    visible = hidden == 0
    bk = _bucket(d)
    for head in range(N_HEADS):
        acc = jnp.zeros((Q_BLOCK, LANE), F32)
        for b in range(N_BUCKETS):
            acc = jnp.where(bk == b, rb_ref[b, head] * LOG2E, acc)
        g = head % GROUP
        o_ref[0, head // GROUP, :, g * Q_BLOCK:(g + 1) * Q_BLOCK] = jnp.where(visible, acc, NEG)


def _bias_prompt(rel_bias, nq):
    return pl.pallas_call(
        functools.partial(_bias_prompt_kernel, nq=nq),
        grid=(nq + 3,),
        in_specs=[pl.BlockSpec(memory_space=pltpu.SMEM)],
        out_specs=pl.BlockSpec((1, N_KV, Q_BLOCK, GROUP * Q_BLOCK), lambda s: (s, 0, 0, 0)),
        out_shape=jax.ShapeDtypeStruct((nq + 3, N_KV, Q_BLOCK, GROUP * Q_BLOCK), F32),
        compiler_params=_params(("arbitrary",)),
        name="bias_prompt",
    )(rel_bias)


def _bias_sample_kernel(rbx_ref, bc_ref, bs_ref, bw_ref, *, s, p, wb):
    def table(o_ref, dfn):
        shape = o_ref.shape
        t = lax.broadcasted_iota(jnp.int32, shape, 0) % s
        k = lax.broadcasted_iota(jnp.int32, shape, 1)
        bk = _bucket(dfn(t, k))
        acc = jnp.zeros(shape, F32)
        for b in range(N_BUCKETS):
            acc = jnp.where(bk == b, rbx_ref[:, b:b + 1], acc)
        o_ref[...] = acc

    table(bc_ref, lambda t, k: p + t - CMP_STRIDE * k - (CMP_BLOCK - 1))
    table(bs_ref, lambda t, k: p + t - k)
    table(bw_ref, lambda t, k: wb + t - k)


def _bias_sample(rbx, s, p, wb, nch, lk, lw):
    nl = N_HEADS * s
    return pl.pallas_call(
        functools.partial(_bias_sample_kernel, s=s, p=p, wb=wb),
        out_shape=[jax.ShapeDtypeStruct((nl, nch), F32),
                   jax.ShapeDtypeStruct((nl, lk), F32),
                   jax.ShapeDtypeStruct((nl, lw), F32)],
        compiler_params=pltpu.CompilerParams(vmem_limit_bytes=VMEM_LIMIT),
        name="bias_sample",
    )(rbx)


def _attn_prompt_kernel(c31_ref, qt_ref, ngt_ref, kcv_ref, ks_ref, vs_ref, kw_ref, vw_ref, tc_ref, t0_ref, t1_ref, t4_ref,
                        c2st_ref, o_ref, q_scr, selb_ref, ocmp_ref, osel_ref, *, nc, nb):
    i = pl.program_id(1)
    width = GROUP * Q_BLOCK
    key = lax.broadcasted_iota(jnp.int32, (Q_BLOCK, width), 0)
    lq = lax.broadcasted_iota(jnp.int32, (Q_BLOCK, width), 1) % Q_BLOCK
    dc = i * Q_BLOCK + lq - CMP_STRIDE * key - (CMP_BLOCK - 1)
    mask_c = (dc >= 0) & (key < nc)

    nbp = selb_ref.shape[1]
    jj = lax.broadcasted_iota(jnp.int32, (nbp, Q_BLOCK), 0)
    qp = i * Q_BLOCK + lax.broadcasted_iota(jnp.int32, (nbp, Q_BLOCK), 1)
    cur = qp // SEL_BLOCK
    valid = (jj * SEL_BLOCK <= qp) & (jj < nb)
    forced = (jj == 0) | (jj == cur) | (jj == cur - 1)
    n_sel = min(N_SEL, nb)
    half = SEL_BLOCK
    heads = range(N_KV)
    hsl = [slice(h * HEAD_DIM, (h + 1) * HEAD_DIM) for h in heads]

    qts, ocmps, selbs = [], [], []
    for h in heads:
        qt = jnp.concatenate(
            [qt_ref[0, (GROUP * h + g) * HEAD_DIM:(GROUP * h + g + 1) * HEAD_DIM, :] for g in range(GROUP)],
            axis=1)
        qts.append((qt * (HEAD_DIM ** -0.5 * LOG2E)).astype(BF16))

    for h in heads:
        s = _dot_tn(kcv_ref[0, hsl[h], :].astype(BF16), qts[h]) + tc_ref[0, h]
        s = jnp.where(mask_c, s, NEG)
        p = jnp.where(mask_c, jnp.exp2(s - jnp.max(s, axis=0, keepdims=True)), 0.0)
        p = p / jnp.maximum(jnp.sum(p, axis=0, keepdims=True), 1e-30)
        ocmps.append(_dot(kcv_ref[0, D_KV + h * HEAD_DIM:D_KV + (h + 1) * HEAD_DIM, :].astype(BF16), p.astype(BF16)))

        psum = p[:, 0:Q_BLOCK]
        for g in range(1, GROUP):
            psum = psum + p[:, g * Q_BLOCK:(g + 1) * Q_BLOCK]
        imp = _dot_hi(c2st_ref[...], psum)
        score = jnp.where(valid, jnp.where(forced, FORCED, imp), -jnp.inf)
        rank = jnp.zeros((nbp, Q_BLOCK), jnp.int32)
        for jp in range(nb):
            rj = score[jp:jp + 1, :]
            beats = (rj > score) | ((rj == score) & (jj > jp))
            rank = rank + beats.astype(jnp.int32)
        selbs.append(jnp.concatenate([jnp.where(rank < n_sel, 0.0, NEG)] * GROUP, axis=1))

    for h in heads:
        q_scr[h] = qts[h]
        ocmp_ref[h] = ocmps[h]
        selb_ref[h] = selbs[h]

    gl = lax.broadcasted_iota(jnp.int32, (1, width), 1) // Q_BLOCK
    cbrow = []
    for h in heads:
        r = jnp.zeros((1, width), F32)
        for g in range(GROUP):
            r = jnp.where(gl == g, c31_ref[GROUP * h + g] * LOG2E, r)
        cbrow.append(r)

    init = tuple((jnp.full((1, width), NEG, F32), jnp.zeros((1, width), F32), jnp.zeros((HEAD_DIM, width), F32))
                 for _ in heads)

    def update(st, sc, v):
        m_old, l_old, acc = st
        m_new = jnp.maximum(m_old, jnp.max(sc, axis=0, keepdims=True))
        pr = jnp.exp2(sc - m_new)
        alpha = jnp.exp2(m_old - m_new)
        return (m_new, alpha * l_old + jnp.sum(pr, axis=0, keepdims=True), alpha * acc + _dot(v, pr.astype(BF16)))

    def tiles(state, k_ref, v_ref, kts, bias_ofs):
        def cat(parts, axis):
            return parts[0] if len(parts) == 1 else jnp.concatenate(parts, axis=axis)

        scs = [cat([bias_of(h, _dot_tn(k_ref[kt, hsl[h], :].astype(BF16), q_scr[h]))
                    for kt, bias_of in zip(kts, bias_ofs)], 0) for h in heads]
        vs = [cat([v_ref[kt, hsl[h], :].astype(BF16) for kt in kts], 1) for h in heads]
        return tuple(update(state[h], scs[h], vs[h]) for h in heads)

    def with_table(t_ref, sel_kt=None):
        def f(h, sc):
            sc = sc + t_ref[0, h]
            if sel_kt is None:
                return sc
            r0 = selb_ref[h, pl.ds(2 * sel_kt, 1), :]
            r1 = selb_ref[h, pl.ds(2 * sel_kt + 1, 1), :]
            return jnp.concatenate([sc[:half] + r0, sc[half:] + r1], axis=0)
        return f

    def far_sel(kt):
        def f(h, sc):
            r0 = selb_ref[h, pl.ds(2 * kt, 1), :] + cbrow[h]
            r1 = selb_ref[h, pl.ds(2 * kt + 1, 1), :] + cbrow[h]
            return jnp.concatenate([sc[:half] + r0, sc[half:] + r1], axis=0)
        return f

    def finish(st):
        return st[2] / jnp.maximum(st[1], 1e-30)

    def when(pred, fn, state):
        return lax.cond(pred, fn, lambda st: st, state)

    nfar = jnp.maximum(i - 1, 0)
    st = lax.fori_loop(
        0, nfar // 2,
        lambda k, st: tiles(st, ks_ref, vs_ref, [2 * k, 2 * k + 1], [far_sel(2 * k), far_sel(2 * k + 1)]), init)
    st = when(nfar % 2 == 1, lambda st: tiles(st, ks_ref, vs_ref, [nfar - 1], [far_sel(nfar - 1)]), st)
    st = lax.cond(
        i >= 1,
        lambda st: tiles(st, ks_ref, vs_ref, [i - 1, i], [with_table(t1_ref, i - 1), with_table(t0_ref, i)]),
        lambda st: tiles(st, ks_ref, vs_ref, [i], [with_table(t0_ref, i)]), st)
    for h in heads:
        osel_ref[h] = finish(st[h])

    far_bias = lambda h, sc: sc + cbrow[h]
    st = lax.cond(
        i >= 1,
        lambda st: tiles(st, kw_ref, vw_ref, [i, i - 1], [with_table(t0_ref), with_table(t1_ref)]),
        lambda st: tiles(st, kw_ref, vw_ref, [i], [with_table(t0_ref)]), init)
    st = lax.cond(
        i >= 3,
        lambda st: tiles(st, kw_ref, vw_ref, [i - 2, i - 3], [far_bias, far_bias]),
        lambda st: when(i >= 2, lambda st: tiles(st, kw_ref, vw_ref, [i - 2], [far_bias]), st), st)
    st = when(i >= 4, lambda st: tiles(st, kw_ref, vw_ref, [i - 4], [with_table(t4_ref)]), st)

    sig = jax.nn.sigmoid(ngt_ref[0])
    for h in heads:
        o_cmp, o_sel, o_win = ocmp_ref[h], osel_ref[h], finish(st[h])
        for g in range(GROUP):
            head = GROUP * h + g
            ls = slice(g * Q_BLOCK, (g + 1) * Q_BLOCK)
            o = (sig[head:head + 1, :] * o_cmp[:, ls] + sig[N_HEADS + head:N_HEADS + head + 1, :] * o_sel[:, ls]
                 + sig[2 * N_HEADS + head:2 * N_HEADS + head + 1, :] * o_win[:, ls])
            o_ref[0, head * HEAD_DIM:(head + 1) * HEAD_DIM, :] = o


def _attn_prompt(qt, ngt, kvt, kcv, tabs, c31, c2st, bsz, nq):
    nch = kcv.shape[2]
    nb = c2st.shape[0]
    width = GROUP * Q_BLOCK
    kvspec = lambda k: pl.BlockSpec((nq, D_KV, LANE), lambda b, i: (b, k, 0))
    tab = lambda k: pl.BlockSpec((1, N_KV, Q_BLOCK, width), lambda b, i: (nq + k, 0, 0, 0))
    return pl.pallas_call(
        functools.partial(_attn_prompt_kernel, nc=nch - 1, nb=nb),
        grid=(bsz, nq),
        in_specs=[pl.BlockSpec(memory_space=pltpu.SMEM),
                  pl.BlockSpec((1, D_ATT, LANE), lambda b, i: (b * nq + i, 0, 0)),
                  pl.BlockSpec((1, NG_ROWS, LANE), lambda b, i: (b * nq + i, 0, 0)),
                  pl.BlockSpec((1, 2 * D_KV, nch), lambda b, i: (b, 0, 0)),
                  kvspec(2), kvspec(3), kvspec(4), kvspec(5),
                  pl.BlockSpec((1, N_KV, Q_BLOCK, width), lambda b, i: (i, 0, 0, 0)),
                  tab(0), tab(1), tab(2),
                  pl.BlockSpec(c2st.shape, lambda b, i: (0, 0))],
        out_specs=pl.BlockSpec((1, D_ATT, LANE), lambda b, i: (b * nq + i, 0, 0)),
        out_shape=jax.ShapeDtypeStruct((bsz * nq, D_ATT, LANE), F32),
        scratch_shapes=[pltpu.VMEM((N_KV, HEAD_DIM, width), BF16),
                        pltpu.VMEM((N_KV, nb, width), F32),
                        pltpu.VMEM((N_KV, HEAD_DIM, width), F32),
                        pltpu.VMEM((N_KV, HEAD_DIM, width), F32)],
        compiler_params=_params(("arbitrary", "arbitrary")),
        name="attn_prompt",
    )(c31, qt, ngt, kcv, kvt, kvt, kvt, kvt, tabs, tabs, tabs, tabs, c2st)


def _softmax_lanes(s, mask):
    s = jnp.where(mask, s, NEG)
    p = jnp.where(mask, jnp.exp(s - jnp.max(s, axis=-1, keepdims=True)), 0.0)
    return p / jnp.maximum(jnp.sum(p, axis=-1, keepdims=True), 1e-30)


def _attn_sample_kernel(pt_ref, *refs, npg, s, p, wb, nb, aliased):
    del pt_ref
    pg_refs = refs[:npg]
    (qbd_ref, ngt_ref, new_ref, cw_ref, bdc_ref, pec_ref, bc_ref, bs_ref, bw_ref,
     c2s_ref, gsum_ref, e_ref) = refs[npg:npg + 12]
    o_ref, win_ref, xs_ref = refs[npg + 12 + int(aliased):]
    nl = N_HEADS * s
    nch = npg * CPP
    n_sel = min(N_SEL, nb)
    qb = (qbd_ref[0] * HEAD_DIM ** -0.5).astype(BF16)

    def row_t(shape):
        return lax.broadcasted_iota(jnp.int32, shape, 0) % s

    def lanes_of(shape):
        return lax.broadcasted_iota(jnp.int32, shape, 1)

    def pad_rows(x):
        return jnp.concatenate([x, jnp.zeros((LANE - x.shape[0], x.shape[1]), x.dtype)], axis=0).astype(BF16)

    kct, vct = _compress_core(lambda sec, pg: pg_refs[pg][0, 0, sec * D_KV:(sec + 1) * D_KV, :],
                              npg, xs_ref, bdc_ref, pec_ref)

    sc = _dot(qb, kct.astype(BF16)) + bc_ref[...]
    c_l = lanes_of((nl, nch))
    dc = p + row_t((nl, nch)) - CMP_STRIDE * c_l - (CMP_BLOCK - 1)
    pc = _softmax_lanes(sc, (dc >= 0) & (c_l < nch - 1))
    o_c = _dot_nt(pc.astype(BF16), vct.astype(BF16))

    imp = _dot_hi(gsum_ref[...], _dot_hi(pc, c2s_ref[...]))
    j_l = lanes_of((nl, LANE))
    qpos = p + row_t((nl, LANE))
    cur = qpos // SEL_BLOCK
    valid = (j_l * SEL_BLOCK <= qpos) & (j_l < nb)
    forced = (j_l == 0) | (j_l == cur) | (j_l == cur - 1)
    score = jnp.where(valid, jnp.where(forced, FORCED, imp), -jnp.inf)
    rank = jnp.zeros((nl, LANE), jnp.int32)
    for jp in range(nb):
        cj = score[:, jp:jp + 1]
        beats = (cj > score) | ((cj == score) & (j_l > jp))
        rank = rank + beats.astype(jnp.int32)
    sel = jnp.where(rank < n_sel, 1.0, 0.0).astype(BF16)
    maskf = _dot(sel, e_ref[...])

    ss = jnp.concatenate([_dot(qb, r[0, 0, 2 * D_KV:3 * D_KV, :].astype(BF16)) for r in pg_refs]
                         + [_dot_nt(qb, pad_rows(new_ref[0, 0]))], axis=1) + bs_ref[...]
    lk = p + LANE
    ds_ = p + row_t((nl, lk)) - lanes_of((nl, lk))
    ps = _softmax_lanes(ss, (maskf > 0.5) & (ds_ >= 0)).astype(BF16)
    o_s = _dot(ps[:, p:], pad_rows(new_ref[0, 1]))
    for g, r in enumerate(pg_refs):
        o_s = o_s + _dot_nt(ps[:, g * PAGE:(g + 1) * PAGE], r[0, 0, 3 * D_KV:4 * D_KV, :].astype(BF16))

    sw = jnp.concatenate([_dot(qb, cw_ref[0, 0, 0:D_KV, :].astype(BF16)),
                          _dot_nt(qb, pad_rows(new_ref[0, 2]))], axis=1) + bw_ref[...]
    lw = wb + LANE
    w_l = lanes_of((nl, lw))
    dw = wb + row_t((nl, lw)) - w_l
    pw = _softmax_lanes(sw, (dw >= 0) & (dw < WINDOW) & (p - wb + w_l >= 0)).astype(BF16)
    o_w = _dot_nt(pw[:, :wb], cw_ref[0, 0, D_KV:2 * D_KV, :].astype(BF16)) + _dot(pw[:, wb:], pad_rows(new_ref[0, 3]))

    g = jax.nn.sigmoid(ngt_ref[0])
    o_ref[0] = g[:, 0:1] * o_c + g[:, 1:2] * o_s + g[:, 2:3] * o_w

    def new_t(x):
        xt = jnp.concatenate([x, jnp.zeros((LANE - x.shape[0], x.shape[1]), x.dtype)], axis=0).T
        return pltpu.roll(xt, LANE - s, 1)

    rolled = pltpu.roll(cw_ref[0, 0], wb - s, 1)
    newt = jnp.concatenate([new_t(new_ref[0, 2]), new_t(new_ref[0, 3])], axis=0)
    lane = lax.broadcasted_iota(jnp.int32, newt.shape, 1)
    last = jnp.where(lane >= LANE - s, newt, rolled[:, wb - LANE:])
    win_ref[0, 0] = jnp.concatenate([rolled[:, :wb - LANE], last], axis=1)


def _attn_sample(cache_t, pt_flat, layer, qbd, ngt, newkv, cwin_t, bdc, pec, bc, bs, bw, c2s, gsum, emat, win_buf,
                 npg, s, p, wb, nb):
    dbs = qbd.shape[0]
    depth = cwin_t.shape[1]
    nl = N_HEADS * s
    nch = npg * CPP
    aliased = win_buf is not None
    pgs = [pl.BlockSpec((1, 1, 4 * D_KV, PAGE), (lambda b, pt, g=g: (pt[b * npg + g], layer, 0, 0)))
           for g in range(npg)]
    const = lambda a: pl.BlockSpec(a.shape, lambda b, pt: (0,) * a.ndim)
    win_spec = pl.BlockSpec((1, 1, 2 * D_KV, wb), lambda b, pt: (b, layer, 0, 0))
    operands = [pt_flat, *([cache_t] * npg), qbd, ngt, newkv, cwin_t, bdc, pec, bc, bs, bw, c2s, gsum, emat]
    in_specs = pgs + [
        pl.BlockSpec((1, nl, D_KV), lambda b, pt: (b, 0, 0)),
        pl.BlockSpec((1, nl, LANE), lambda b, pt: (b, 0, 0)),
        pl.BlockSpec((1,) + newkv.shape[1:], lambda b, pt: (b, 0, 0, 0)),
        win_spec,
        const(bdc), const(pec), const(bc), const(bs), const(bw), const(c2s), const(gsum), const(emat)]
    aliases = {}
    if aliased:
        aliases = {len(operands): 1}
        operands.append(win_buf)
        in_specs.append(pl.BlockSpec(memory_space=pl.ANY))
    grid_spec = pltpu.PrefetchScalarGridSpec(
        num_scalar_prefetch=1,
        grid=(dbs,),
        in_specs=in_specs,
        out_specs=[pl.BlockSpec((1, nl, D_KV), lambda b, pt: (b, 0, 0)), win_spec],
        scratch_shapes=[pltpu.VMEM((2, nch * XS_PITCH, LANE), F32)],
    )
    return pl.pallas_call(
        functools.partial(_attn_sample_kernel, npg=npg, s=s, p=p, wb=wb, nb=nb, aliased=aliased),
        grid_spec=grid_spec,
        out_shape=[jax.ShapeDtypeStruct((dbs, nl, D_KV), F32),
                   jax.ShapeDtypeStruct((dbs, depth, 2 * D_KV, wb), F32)],
        input_output_aliases=aliases,
        compiler_params=_params(("arbitrary",)),
        name="attn_sample",
    )(*operands)


def _merge_kernel(h_ref, ga_ref, att_ref, bz_ref, ma_ref, mb_ref, wa_ref, wb_ref, wo_ref, fg_ref, o_ref, *, final):
    y_a = _dot(ga_ref[...], wa_ref[...])
    xt = jnp.concatenate([att_ref[k] * _silu(bz_ref[k]) for k in range(att_ref.shape[0])], axis=1)
    y_b = _dot_tn(xt.astype(BF16), wb_ref[...])
    mix = jax.nn.sigmoid(ma_ref[...]) * y_a + jax.nn.sigmoid(mb_ref[...]) * y_b
    out = h_ref[...] + _dot(mix.astype(BF16), wo_ref[...])
    if final:
        ms = jnp.mean(out * out, axis=-1, keepdims=True)
        out = out * lax.rsqrt(ms + EPS) * fg_ref[...]
    o_ref[...] = out


def _merge(h, ga, att_t, bz_t, pm, wa, wb, wo, fg, tm, final):
    n = h.shape[0]
    tl = tm // LANE
    row = pl.BlockSpec((tm, 1024), lambda i: (i, 0))
    col = lambda k: pl.BlockSpec((tm, 1024), lambda i: (i, k))
    tsp = pl.BlockSpec((tl, D_ATT, LANE), lambda i: (i, 0, 0))
    wsp = pl.BlockSpec((1024, 1024), lambda i: (0, 0))
    return pl.pallas_call(
        functools.partial(_merge_kernel, final=final),
        grid=(n // tm,),
        in_specs=[row, row, tsp, tsp, col(COL_MA), col(COL_MB),
                  wsp, wsp, wsp, pl.BlockSpec((1, 1024), lambda i: (0, 0))],
        out_specs=row,
        out_shape=jax.ShapeDtypeStruct((n, D_MODEL), F32),
        compiler_params=_params(("arbitrary",)),
        name="merge",
    )(h, ga, att_t, bz_t, pm, pm, wa, wb, wo, fg)


def _cmp_to_sel(nc, nb):
    c0 = np.arange(nc)[:, None] * CMP_STRIDE
    j0 = np.arange(nb)[None, :] * SEL_BLOCK
    ov = np.minimum(c0 + CMP_BLOCK, j0 + SEL_BLOCK) - np.maximum(c0, j0)
    return np.maximum(ov, 0).astype(np.float32) / CMP_BLOCK


def _largest_divisor(n, cap, mult=1):
    for d in range(min(n, cap), 0, -1):
        if n % d == 0 and d % mult == 0:
            return d
    raise ValueError("no tile size")


def _tokens_major(xt):
    return xt.transpose(0, 2, 1).reshape(xt.shape[0] * LANE, xt.shape[1])


def kernel(x_prompt, x_sample, cache_kv, cache_win, state_conv, page_table, rel_bias, norm_g, w_in, conv_w,
           w_cmp, pe_cmp, w_a_out, w_b_out, w_o, final_g):
    bsz, t, _ = x_prompt.shape
    dbs, s, _ = x_sample.shape
    depth = w_in.shape[0]
    n_pool = cache_kv.shape[0]
    npg = page_table.shape[1]
    p = npg * PAGE
    wb = cache_win.shape[2]
    nq = t // Q_BLOCK
    nch_p = t // CMP_STRIDE
    nch_s = (p + s) // CMP_STRIDE
    nb_p = -(-t // SEL_BLOCK)
    nb_s = -(-(p + s) // SEL_BLOCK)
    ns = dbs * s
    assert t % Q_BLOCK == 0 and nq % 2 == 0 and nch_p == LANE and nb_p % 8 == 0, "prompt length not supported"
    assert 2 <= s <= 8 and wb == min(WINDOW, p) and nch_s == p // CMP_STRIDE == LANE and nb_s <= LANE
    assert ns % LANE == 0 and wb == WINDOW and wb % LANE == 0, "sample shape not supported"
    assert WINDOW == 4 * Q_BLOCK and SEL_BLOCK * 2 == Q_BLOCK
    nl = N_HEADS * s
    lk, lw = p + LANE, wb + LANE

    ng_perm = np.array([_O_NG + head * 3 + br for br in range(3) for head in range(N_HEADS)])
    perm_nat = np.concatenate([np.arange(0, _O_Q), np.arange(_O_MA, _O_END)])
    perm_tr = np.concatenate([np.arange(_O_Q, _O_KC), np.arange(_O_BZ, _O_MA), np.arange(_O_KC, _O_NG), ng_perm])
    w_nat = w_in[:, :, perm_nat].astype(BF16)
    w_tr = jnp.pad(w_in.transpose(0, 2, 1)[:, perm_tr, :],
                   ((0, 0), (0, D_TR - perm_tr.size), (0, 0))).astype(BF16)
    wa_b, wb_b, wo_b = w_a_out.astype(BF16), w_b_out.astype(BF16), w_o.astype(BF16)
    eye_kv = jnp.eye(N_KV, dtype=F32)
    bd = jnp.einsum('lsjde,hk->lsjhdke', w_cmp, eye_kv).reshape(depth, 2, CMP_BLOCK, D_KV, D_KV).astype(BF16)
    bdc = jnp.concatenate([bd[:, :, :CMP_STRIDE], bd[:, :, CMP_STRIDE:]], axis=-1)
    pe_t = jnp.tile(pe_cmp, (1, 1, 1, N_KV))
    pec = _pe_const(bdc, pe_t)
    g2 = norm_g.reshape(depth, 1, D_MODEL)
    fg2 = final_g.reshape(1, D_MODEL)

    tabs = _bias_prompt(rel_bias, nq)
    c31 = rel_bias[N_BUCKETS - 1]
    rbx = jnp.repeat(rel_bias.T, s, axis=0)
    bc, bs, bw = _bias_sample(rbx, s, p, wb, nch_s, lk, lw)
    c2st_p = np.zeros((nb_p, LANE), np.float32)
    c2st_p[:, :nch_p - 1] = _cmp_to_sel(nch_p - 1, nb_p).T
    c2s_s = np.zeros((LANE, LANE), np.float32)
    c2s_s[:nch_s - 1, :nb_s] = _cmp_to_sel(nch_s - 1, nb_s)
    rl = np.arange(nl)
    gsum = ((rl[:, None] // (GROUP * s) == rl[None, :] // (GROUP * s))
            & (rl[:, None] % s == rl[None, :] % s)).astype(np.float32)
    emat_s = (np.arange(lk)[None, :] // SEL_BLOCK == np.arange(LANE)[:, None])
    c2st_p, c2s_s, gsum = jnp.asarray(c2st_p), jnp.asarray(c2s_s), jnp.asarray(gsum)
    emat_s = jnp.asarray(emat_s, dtype=BF16)

    cache_t = cache_kv.transpose(0, 1, 3, 4, 5, 2).reshape(n_pool, depth, 4 * D_KV, PAGE)
    cwin_t = cache_win.transpose(0, 1, 3, 4, 5, 2).reshape(dbs, depth, 2 * D_KV, wb)
    pt_flat = page_table.reshape(-1).astype(jnp.int32)

    tm_p = _largest_divisor(bsz * t, 1024, LANE)
    tt = _largest_divisor(t, 512, 8)
    tm_m = _largest_divisor(bsz * t, 512, LANE)

    hp = x_prompt.reshape(bsz * t, D_MODEL)
    hs = x_sample.reshape(ns, D_MODEL)
    kv_p, kv_s, win_p, conv_p, conv_s = [], [], [], [], []
    win_buf = None
    zs = jnp.zeros((dbs, s - 1, D_CONV), F32)
    for l in range(depth):
        final = l == depth - 1
        pm, qt, bzt, kvt, ngt = _inproj(hp, g2[l], w_nat[l], w_tr[l], tm_p)
        ga, tail = _conv_prompt(pm.reshape(bsz, t, D_NAT), conv_w[l], tt)
        kcv = _compress_prompt(kvt, bdc[l], pec[l], bsz, nq)
        att_t = _attn_prompt(qt, ngt, kvt, kcv, tabs, c31, c2st_p, bsz, nq)
        hp = _merge(hp, ga.reshape(bsz * t, D_CONV), att_t, bzt, pm, wa_b[l], wb_b[l], wo_b[l], fg2, tm_m, final)
        kv_nat = _tokens_major(kvt).reshape(bsz, t, 6, N_KV, HEAD_DIM)
        kv_p.append(kv_nat[:, :, :4])
        win_p.append(kv_nat[:, t - min(WINDOW, t):, 4:])
        conv_p.append(tail[:, 8 - (CONV_W - 1):])

        pm_s, qt_s, bzt_s, kvt_s, ngt_s = _inproj(hs, g2[l], w_nat[l], w_tr[l], ns)
        st = state_conv[:, l]
        s1 = jnp.concatenate([st[:, 1:2], zs], axis=1).reshape(ns, D_CONV)
        s2 = jnp.concatenate([st, zs[:, :s - 2]], axis=1).reshape(ns, D_CONV)
        ga_s, u_s = _conv_sample(pm_s, s1, s2, conv_w[l], s)
        q_s = _tokens_major(qt_s).reshape(dbs, s, N_KV, GROUP, HEAD_DIM)
        q_r = q_s.transpose(0, 2, 3, 1, 4).reshape(dbs, N_KV, GROUP * s, HEAD_DIM)
        qbd = jnp.einsum('bkmd,kj->bkmjd', q_r, eye_kv).reshape(dbs, nl, D_KV)
        ng_s = _tokens_major(ngt_s)[:, :3 * N_HEADS].reshape(dbs, s, 3, N_HEADS).transpose(0, 3, 1, 2)
        ng_s = jnp.pad(ng_s.reshape(dbs, nl, 3), ((0, 0), (0, 0), (0, LANE - 3)))
        kv_nat_s = _tokens_major(kvt_s).reshape(dbs, s, 6, D_KV)
        newkv = jnp.pad(kv_nat_s[:, :, 2:].transpose(0, 2, 1, 3), ((0, 0), (0, 0), (0, 8 - s), (0, 0)))
        o_s, win_buf = _attn_sample(cache_t, pt_flat, l, qbd, ng_s, newkv, cwin_t, bdc[l], pec[l], bc, bs, bw,
                                    c2s_s, gsum, emat_s, win_buf, npg, s, p, wb, nb_s)
        o6 = o_s.reshape(dbs, N_KV, GROUP, s, N_KV, HEAD_DIM)
        att_s = jnp.stack([o6[:, k, :, :, k] for k in range(N_KV)], axis=1)
        att_s = att_s.transpose(0, 3, 1, 2, 4).reshape(ns // LANE, LANE, D_ATT).transpose(0, 2, 1)
        hs = _merge(hs, ga_s, att_s, bzt_s, pm_s, wa_b[l], wb_b[l], wo_b[l], fg2, ns, final)
        kv_s.append(kv_nat_s[:, :, :4].reshape(dbs, s, 4, N_KV, HEAD_DIM))
        conv_s.append(u_s.reshape(dbs, s, D_CONV)[:, s - (CONV_W - 1):])

    y_prompt = hp.reshape(bsz, t, D_MODEL)
    y_sample = hs.reshape(dbs, s, D_MODEL)
    win_sample = win_buf.reshape(dbs, depth, 2, N_KV, HEAD_DIM, wb).transpose(0, 1, 5, 2, 3, 4)
    return (y_prompt, y_sample, jnp.stack(kv_p, axis=1), jnp.stack(kv_s, axis=1), jnp.stack(win_p, axis=1),
            win_sample, jnp.stack(conv_p, axis=1), jnp.stack(conv_s, axis=1))
```

```python
import functools
import math

import numpy as np
import jax
import jax.numpy as jnp
from jax import lax
from jax.experimental import pallas as pl
from jax.experimental.pallas import tpu as pltpu

F32 = jnp.float32
BF16 = jnp.bfloat16

D_MODEL = 1024
N_HEADS = 16
N_KV = 4
GROUP = N_HEADS // N_KV
HEAD_DIM = 64
D_ATT = N_HEADS * HEAD_DIM
D_KV = N_KV * HEAD_DIM
D_CONV = D_MODEL
CONV_W = 3
CMP_BLOCK = 32
CMP_STRIDE = 16
SEL_BLOCK = 64
N_SEL = 8
WINDOW = 512
Q_BLOCK = 128
FORCED = 1e4
N_BUCKETS = 32
MAX_DISTANCE = 128
EPS = 1e-6
NEG = -1e30
LOG2E = math.log2(math.e)
PAGE = 128
CPP = PAGE // CMP_STRIDE

LANE = 128
TN = 512
TN_NAT = 1024
COL_B, COL_C, COL_H, COL_ZA, COL_MA, COL_MB = range(6)
D_NAT = 6 * 1024
D_TR = 8 * TN
NG_ROWS = 64
_O_Q, _O_KC, _O_NG, _O_BZ, _O_MA, _O_END = 4096, 5120, 6656, 6704, 7728, 9776

VMEM_LIMIT = 56 * 1024 * 1024
XS_PITCH = 24

_EXACT = N_BUCKETS // 2
_THRESH = tuple(int(math.ceil(_EXACT * (MAX_DISTANCE / _EXACT) ** (k / (N_BUCKETS - _EXACT))))
                for k in range(1, N_BUCKETS - _EXACT))


def _params(sem, vmem=VMEM_LIMIT):
    return pltpu.CompilerParams(dimension_semantics=sem, vmem_limit_bytes=vmem)


def _bucket(d):
    d = jnp.maximum(d, 0)
    big = jnp.full(d.shape, _EXACT, jnp.int32)
    for thr in _THRESH:
        big = big + (d >= thr).astype(jnp.int32)
    return jnp.where(d < _EXACT, d, big)


def _dot(a, b):
    return jnp.dot(a, b, preferred_element_type=F32)


def _dot_nt(a, b):
    return lax.dot_general(a, b, (((1,), (1,)), ((), ())), preferred_element_type=F32)


def _dot_tn(a, b):
    return lax.dot_general(a, b, (((0,), (0,)), ((), ())), preferred_element_type=F32)


def _dot_hi(a, b):
    return jnp.dot(a, b, preferred_element_type=F32, precision=lax.Precision.HIGHEST)


def _silu(z):
    return z * jax.nn.sigmoid(z)


def _inproj_kernel(x_ref, g_ref, w_ref, wt_ref, main_ref, qt_ref, bzt_ref, kvt_ref, ngt_ref, xn_ref, *, n_nat):
    j = pl.program_id(1)
    ntile = qt_ref.shape[0]

    @pl.when(j == 0)
    def _():
        x = x_ref[...]
        ms = jnp.mean(x * x, axis=-1, keepdims=True)
        xn_ref[...] = (x * lax.rsqrt(ms + EPS) * g_ref[...]).astype(BF16)

    @pl.when(j < n_nat)
    def _():
        main_ref[...] = _dot(xn_ref[...], w_ref[...])

    @pl.when(j >= n_nat)
    def _():
        yt = _dot_nt(wt_ref[...], xn_ref[...])

        def put(ref, rows):
            for k in range(ntile):
                ref[k] = yt[:rows, k * LANE:(k + 1) * LANE]

        jt = j - n_nat

        @pl.when(jt < 2)
        def _():
            put(qt_ref, TN)

        @pl.when((jt >= 2) & (jt < 4))
        def _():
            put(bzt_ref, TN)

        @pl.when((jt >= 4) & (jt < 7))
        def _():
            put(kvt_ref, TN)

        @pl.when(jt == 7)
        def _():
            put(ngt_ref, NG_ROWS)


def _inproj(x, g, w_nat, w_tr, tm):
    n = x.shape[0]
    n_nat = D_NAT // TN_NAT
    nt = n // LANE
    tl = tm // LANE
    clampi = lambda v, lo, hi: jnp.minimum(jnp.maximum(v, lo), hi)
    return pl.pallas_call(
        functools.partial(_inproj_kernel, n_nat=n_nat),
        grid=(n // tm, n_nat + D_TR // TN),
        in_specs=[pl.BlockSpec((tm, D_MODEL), lambda i, j: (i, 0)),
                  pl.BlockSpec((1, D_MODEL), lambda i, j: (0, 0)),
                  pl.BlockSpec((D_MODEL, TN_NAT), lambda i, j: (0, jnp.minimum(j, n_nat - 1))),
                  pl.BlockSpec((TN, D_MODEL), lambda i, j: (jnp.maximum(j - n_nat, 0), 0))],
        out_specs=[pl.BlockSpec((tm, TN_NAT), lambda i, j: (i, jnp.minimum(j, n_nat - 1))),
                   pl.BlockSpec((tl, TN, LANE), lambda i, j: (i, clampi(j - n_nat, 0, 1), 0)),
                   pl.BlockSpec((tl, TN, LANE), lambda i, j: (i, clampi(j - n_nat - 2, 0, 1), 0)),
                   pl.BlockSpec((tl, TN, LANE), lambda i, j: (i, clampi(j - n_nat - 4, 0, 2), 0)),
                   pl.BlockSpec((tl, NG_ROWS, LANE), lambda i, j: (i, 0, 0))],
        out_shape=[jax.ShapeDtypeStruct((n, D_NAT), F32),
                   jax.ShapeDtypeStruct((nt, D_ATT, LANE), F32),
                   jax.ShapeDtypeStruct((nt, D_ATT, LANE), F32),
                   jax.ShapeDtypeStruct((nt, 6 * D_KV, LANE), F32),
                   jax.ShapeDtypeStruct((nt, NG_ROWS, LANE), F32)],
        scratch_shapes=[pltpu.VMEM((tm, D_MODEL), BF16)],
        compiler_params=_params(("arbitrary", "arbitrary")),
        name="inproj",
    )(x, g, w_nat, w_tr)


def _conv_prompt_kernel(b_ref, c_ref, h_ref, z_ref, cp_ref, hp_ref, w_ref, ga_ref, tail_ref):
    t = pl.program_id(1)
    u = c_ref[0] * h_ref[0]
    up = jnp.where(t > 0, cp_ref[0] * hp_ref[0], 0.0)
    tt = u.shape[0]
    row = lax.broadcasted_iota(jnp.int32, u.shape, 0)
    um1 = jnp.where(row == 0, up[7:8], pltpu.roll(u, 1, 0))
    um2 = jnp.where(row == 0, up[6:7], jnp.where(row == 1, up[7:8], pltpu.roll(u, 2, 0)))
    w = w_ref[...]
    conv = w[0:1] * um2 + w[1:2] * um1 + w[2:3] * u
    ga_ref[0] = (b_ref[0] * conv * _silu(z_ref[0])).astype(BF16)

    @pl.when(t == pl.num_programs(1) - 1)
    def _():
        tail_ref[0] = u[tt - 8:]


def _conv_prompt(p3, conv_w, tt):
    bsz, t, _ = p3.shape
    hb = tt // 8
    col = lambda k: (lambda b, i: (b, i, k))
    halo = lambda k: (lambda b, i: (b, jnp.maximum(i * hb - 1, 0), k))
    return pl.pallas_call(
        _conv_prompt_kernel,
        grid=(bsz, t // tt),
        in_specs=[pl.BlockSpec((1, tt, 1024), col(COL_B)),
                  pl.BlockSpec((1, tt, 1024), col(COL_C)),
                  pl.BlockSpec((1, tt, 1024), col(COL_H)),
                  pl.BlockSpec((1, tt, 1024), col(COL_ZA)),
                  pl.BlockSpec((1, 8, 1024), halo(COL_C)),
                  pl.BlockSpec((1, 8, 1024), halo(COL_H)),
                  pl.BlockSpec((CONV_W, D_CONV), lambda b, i: (0, 0))],
        out_specs=[pl.BlockSpec((1, tt, 1024), lambda b, i: (b, i, 0)),
                   pl.BlockSpec((1, 8, 1024), lambda b, i: (b, 0, 0))],
        out_shape=[jax.ShapeDtypeStruct((bsz, t, D_CONV), BF16),
                   jax.ShapeDtypeStruct((bsz, 8, D_CONV), F32)],
        compiler_params=_params(("arbitrary", "arbitrary")),
        name="conv_prompt",
    )(p3, p3, p3, p3, p3, p3, conv_w)


def _conv_sample_kernel(b_ref, c_ref, h_ref, z_ref, s1_ref, s2_ref, w_ref, ga_ref, u_ref, *, s):
    u = c_ref[...] * h_ref[...]
    t = lax.broadcasted_iota(jnp.int32, u.shape, 0) % s
    um1 = jnp.where(t >= 1, pltpu.roll(u, 1, 0), s1_ref[...])
    um2 = jnp.where(t >= 2, pltpu.roll(u, 2, 0), s2_ref[...])
    w = w_ref[...]
    conv = w[0:1] * um2 + w[1:2] * um1 + w[2:3] * u
    ga_ref[...] = (b_ref[...] * conv * _silu(z_ref[...])).astype(BF16)
    u_ref[...] = u


def _conv_sample(p2, s1, s2, conv_w, s):
    n = p2.shape[0]
    col = lambda k: pl.BlockSpec((n, 1024), lambda i: (0, k))
    full = pl.BlockSpec((n, 1024), lambda i: (0, 0))
    return pl.pallas_call(
        functools.partial(_conv_sample_kernel, s=s),
        grid=(1,),
        in_specs=[col(COL_B), col(COL_C), col(COL_H), col(COL_ZA), full, full,
                  pl.BlockSpec((CONV_W, D_CONV), lambda i: (0, 0))],
        out_specs=[full, full],
        out_shape=[jax.ShapeDtypeStruct((n, D_CONV), BF16),
                   jax.ShapeDtypeStruct((n, D_CONV), F32)],
        compiler_params=_params(("arbitrary",)),
        name="conv_sample",
    )(p2, p2, p2, p2, s1, s2, conv_w)


def _compress_core(get_page, npg, xs_ref, bdc_ref, pec_ref):
    nch = npg * CPP
    outs = []
    for sec in range(2):
        for pg in range(npg):
            x = get_page(sec, pg).T
            for half in range(2):
                for c in range(CPP):
                    r0 = (pg * CPP + c) * XS_PITCH
                    xs_ref[half, r0:r0 + CMP_STRIDE, :] = x[c * CMP_STRIDE:(c + 1) * CMP_STRIDE,
                                                            half * LANE:(half + 1) * LANE]
        acc = jnp.zeros((nch, 2 * D_KV), F32)
        for j in range(CMP_STRIDE):
            xj = jnp.concatenate([xs_ref[half, pl.ds(j, nch, stride=XS_PITCH), :] for half in range(2)], axis=1)
            acc = acc + _dot(xj.astype(BF16), bdc_ref[sec, j])
        acc = acc + pec_ref[sec, 0:1, :]
        out = acc[:, :D_KV] + pltpu.roll(acc[:, D_KV:], nch - 1, 0)
        row = lax.broadcasted_iota(jnp.int32, out.shape, 0)
        out = jnp.where(row == nch - 1, 0.0, out)
        outs.append(out.T)
    return outs


def _pe_const_kernel(bdc_ref, pe_ref, o_ref):
    for sec in range(2):
        acc = jnp.zeros((8, 2 * D_KV), F32)
        for j in range(CMP_STRIDE):
            w = bdc_ref[0, sec, j]
            lo = _dot(jnp.broadcast_to(pe_ref[0, sec, j:j + 1, :], (8, D_KV)).astype(BF16), w[:, :D_KV])
            hi = _dot(jnp.broadcast_to(pe_ref[0, sec, CMP_STRIDE + j:CMP_STRIDE + j + 1, :], (8, D_KV)).astype(BF16),
                      w[:, D_KV:])
            acc = acc + jnp.concatenate([lo, hi], axis=1)
        o_ref[0, sec] = acc


def _pe_const(bdc, pe):
    depth = bdc.shape[0]
    return pl.pallas_call(
        _pe_const_kernel,
        grid=(depth,),
        in_specs=[pl.BlockSpec((1,) + bdc.shape[1:], lambda l: (l, 0, 0, 0, 0)),
                  pl.BlockSpec((1,) + pe.shape[1:], lambda l: (l, 0, 0, 0))],
        out_specs=pl.BlockSpec((1, 2, 8, 2 * D_KV), lambda l: (l, 0, 0, 0)),
        out_shape=jax.ShapeDtypeStruct((depth, 2, 8, 2 * D_KV), F32),
        compiler_params=_params(("arbitrary",)),
        name="pe_const",
    )(bdc, pe)


def _compress_prompt_kernel(kv_ref, bdc_ref, pec_ref, o_ref, xs_ref):
    npg = kv_ref.shape[0]
    kct, vct = _compress_core(lambda sec, pg: kv_ref[pg, sec * D_KV:(sec + 1) * D_KV, :], npg, xs_ref, bdc_ref, pec_ref)
    o_ref[0, 0:D_KV, :] = kct
    o_ref[0, D_KV:2 * D_KV, :] = vct


def _compress_prompt(kvt, bdc, pec, bsz, nq):
    nch = nq * CPP
    return pl.pallas_call(
        _compress_prompt_kernel,
        grid=(bsz,),
        in_specs=[pl.BlockSpec((nq, 2 * D_KV, LANE), lambda b: (b, 0, 0)),
                  pl.BlockSpec(bdc.shape, lambda b: (0, 0, 0, 0)),
                  pl.BlockSpec(pec.shape, lambda b: (0, 0, 0))],
        out_specs=pl.BlockSpec((1, 2 * D_KV, nch), lambda b: (b, 0, 0)),
        out_shape=jax.ShapeDtypeStruct((bsz, 2 * D_KV, nch), F32),
        scratch_shapes=[pltpu.VMEM((2, nch * XS_PITCH, LANE), F32)],
        compiler_params=_params(("arbitrary",)),
        name="compress_prompt",
    )(kvt, bdc, pec)


def _bias_prompt_kernel(rb_ref, o_ref, *, nq):
    s = pl.program_id(0)
    r = lax.broadcasted_iota(jnp.int32, (Q_BLOCK, LANE), 0)
    c = lax.broadcasted_iota(jnp.int32, (Q_BLOCK, LANE), 1)
    d_cmp = s * Q_BLOCK + c - CMP_STRIDE * r - (CMP_BLOCK - 1)
    d = jnp.where(s < nq, d_cmp, jnp.where(s == nq, c - r, jnp.where(s == nq + 1, Q_BLOCK + c - r, WINDOW + c - r)))
    hidden = jnp.where(s == nq, (c < r).astype(jnp.int32), jnp.where(s == nq + 2, (r <= c).astype(jnp.int32), 0))
    visible = hidden == 0
    bk = _bucket(d)
    for head in range(N_HEADS):
        acc = jnp.zeros((Q_BLOCK, LANE), F32)
        for b in range(N_BUCKETS):
            acc = jnp.where(bk == b, rb_ref[b, head] * LOG2E, acc)
        g = head % GROUP
        o_ref[0, head // GROUP, :, g * Q_BLOCK:(g + 1) * Q_BLOCK] = jnp.where(visible, acc, NEG)


def _bias_prompt(rel_bias, nq):
    return pl.pallas_call(
        functools.partial(_bias_prompt_kernel, nq=nq),
        grid=(nq + 3,),
        in_specs=[pl.BlockSpec(memory_space=pltpu.SMEM)],
        out_specs=pl.BlockSpec((1, N_KV, Q_BLOCK, GROUP * Q_BLOCK), lambda s: (s, 0, 0, 0)),
        out_shape=jax.ShapeDtypeStruct((nq + 3, N_KV, Q_BLOCK, GROUP * Q_BLOCK), F32),
        compiler_params=_params(("arbitrary",)),
        name="bias_prompt",
    )(rel_bias)


def _bias_sample_kernel(rbx_ref, bc_ref, bs_ref, bw_ref, *, s, p, wb):
    def table(o_ref, dfn):
        shape = o_ref.shape
        t = lax.broadcasted_iota(jnp.int32, shape, 0) % s
        k = lax.broadcasted_iota(jnp.int32, shape, 1)
        bk = _bucket(dfn(t, k))
        acc = jnp.zeros(shape, F32)
        for b in range(N_BUCKETS):
            acc = jnp.where(bk == b, rbx_ref[:, b:b + 1], acc)
        o_ref[...] = acc

    table(bc_ref, lambda t, k: p + t - CMP_STRIDE * k - (CMP_BLOCK - 1))
    table(bs_ref, lambda t, k: p + t - k)
    table(bw_ref, lambda t, k: wb + t - k)


def _bias_sample(rbx, s, p, wb, nch, lk, lw):
    nl = N_HEADS * s
    return pl.pallas_call(
        functools.partial(_bias_sample_kernel, s=s, p=p, wb=wb),
        out_shape=[jax.ShapeDtypeStruct((nl, nch), F32),
                   jax.ShapeDtypeStruct((nl, lk), F32),
                   jax.ShapeDtypeStruct((nl, lw), F32)],
        compiler_params=pltpu.CompilerParams(vmem_limit_bytes=VMEM_LIMIT),
        name="bias_sample",
    )(rbx)


def _attn_prompt_kernel(c31_ref, qt_ref, ngt_ref, kcv_ref, ks_ref, vs_ref, kw_ref, vw_ref, tc_ref, t0_ref, t1_ref, t4_ref,
                        c2st_ref, o_ref, q_scr, selb_ref, ocmp_ref, osel_ref, *, nc, nb):
    i = pl.program_id(1)
    width = GROUP * Q_BLOCK
    key = lax.broadcasted_iota(jnp.int32, (Q_BLOCK, width), 0)
    lq = lax.broadcasted_iota(jnp.int32, (Q_BLOCK, width), 1) % Q_BLOCK
    dc = i * Q_BLOCK + lq - CMP_STRIDE * key - (CMP_BLOCK - 1)
    mask_c = (dc >= 0) & (key < nc)

    nbp = selb_ref.shape[1]
    jj = lax.broadcasted_iota(jnp.int32, (nbp, Q_BLOCK), 0)
    qp = i * Q_BLOCK + lax.broadcasted_iota(jnp.int32, (nbp, Q_BLOCK), 1)
    cur = qp // SEL_BLOCK
    valid = (jj * SEL_BLOCK <= qp) & (jj < nb)
    forced = (jj == 0) | (jj == cur) | (jj == cur - 1)
    n_sel = min(N_SEL, nb)
    half = SEL_BLOCK
    heads = range(N_KV)
    hsl = [slice(h * HEAD_DIM, (h + 1) * HEAD_DIM) for h in heads]

    qts, ocmps, selbs = [], [], []
    for h in heads:
        qt = jnp.concatenate(
            [qt_ref[0, (GROUP * h + g) * HEAD_DIM:(GROUP * h + g + 1) * HEAD_DIM, :] for g in range(GROUP)],
            axis=1)
        qts.append((qt * (HEAD_DIM ** -0.5 * LOG2E)).astype(BF16))

    for h in heads:
        s = _dot_tn(kcv_ref[0, hsl[h], :].astype(BF16), qts[h]) + tc_ref[0, h]
        s = jnp.where(mask_c, s, NEG)
        p = jnp.where(mask_c, jnp.exp2(s - jnp.max(s, axis=0, keepdims=True)), 0.0)
        p = p / jnp.maximum(jnp.sum(p, axis=0, keepdims=True), 1e-30)
        ocmps.append(_dot(kcv_ref[0, D_KV + h * HEAD_DIM:D_KV + (h + 1) * HEAD_DIM, :].astype(BF16), p.astype(BF16)))

        psum = p[:, 0:Q_BLOCK]
        for g in range(1, GROUP):
            psum = psum + p[:, g * Q_BLOCK:(g + 1) * Q_BLOCK]
        imp = _dot_hi(c2st_ref[...], psum)
        score = jnp.where(valid, jnp.where(forced, FORCED, imp), -jnp.inf)
        rank = jnp.zeros((nbp, Q_BLOCK), jnp.int32)
        for jp in range(nb):
            rj = score[jp:jp + 1, :]
            beats = (rj > score) | ((rj == score) & (jj > jp))
            rank = rank + beats.astype(jnp.int32)
        selbs.append(jnp.concatenate([jnp.where(rank < n_sel, 0.0, NEG)] * GROUP, axis=1))

    for h in heads:
        q_scr[h] = qts[h]
        ocmp_ref[h] = ocmps[h]
        selb_ref[h] = selbs[h]

    gl = lax.broadcasted_iota(jnp.int32, (1, width), 1) // Q_BLOCK
    cbrow = []
    for h in heads:
        r = jnp.zeros((1, width), F32)
        for g in range(GROUP):
            r = jnp.where(gl == g, c31_ref[GROUP * h + g] * LOG2E, r)
        cbrow.append(r)

    init = tuple((jnp.full((1, width), NEG, F32), jnp.zeros((1, width), F32), jnp.zeros((HEAD_DIM, width), F32))
                 for _ in heads)

    def update(st, sc, v):
        m_old, l_old, acc = st
        m_new = jnp.maximum(m_old, jnp.max(sc, axis=0, keepdims=True))
        pr = jnp.exp2(sc - m_new)
        alpha = jnp.exp2(m_old - m_new)
        return (m_new, alpha * l_old + jnp.sum(pr, axis=0, keepdims=True), alpha * acc + _dot(v, pr.astype(BF16)))

    def tiles(state, k_ref, v_ref, kts, bias_ofs):
        def cat(parts, axis):
            return parts[0] if len(parts) == 1 else jnp.concatenate(parts, axis=axis)

        scs = [cat([bias_of(h, _dot_tn(k_ref[kt, hsl[h], :].astype(BF16), q_scr[h]))
                    for kt, bias_of in zip(kts, bias_ofs)], 0) for h in heads]
        vs = [cat([v_ref[kt, hsl[h], :].astype(BF16) for kt in kts], 1) for h in heads]
        return tuple(update(state[h], scs[h], vs[h]) for h in heads)

    def with_table(t_ref, sel_kt=None):
        def f(h, sc):
            sc = sc + t_ref[0, h]
            if sel_kt is None:
                return sc
            r0 = selb_ref[h, pl.ds(2 * sel_kt, 1), :]
            r1 = selb_ref[h, pl.ds(2 * sel_kt + 1, 1), :]
            return jnp.concatenate([sc[:half] + r0, sc[half:] + r1], axis=0)
        return f

    def far_sel(kt):
        def f(h, sc):
            r0 = selb_ref[h, pl.ds(2 * kt, 1), :] + cbrow[h]
            r1 = selb_ref[h, pl.ds(2 * kt + 1, 1), :] + cbrow[h]
            return jnp.concatenate([sc[:half] + r0, sc[half:] + r1], axis=0)
        return f

    def finish(st):
        return st[2] / jnp.maximum(st[1], 1e-30)

    def when(pred, fn, state):
        return lax.cond(pred, fn, lambda st: st, state)

    nfar = jnp.maximum(i - 1, 0)
    st = lax.fori_loop(
        0, nfar // 2,
        lambda k, st: tiles(st, ks_ref, vs_ref, [2 * k, 2 * k + 1], [far_sel(2 * k), far_sel(2 * k + 1)]), init)
    st = when(nfar % 2 == 1, lambda st: tiles(st, ks_ref, vs_ref, [nfar - 1], [far_sel(nfar - 1)]), st)
    st = lax.cond(
        i >= 1,
        lambda st: tiles(st, ks_ref, vs_ref, [i - 1, i], [with_table(t1_ref, i - 1), with_table(t0_ref, i)]),
        lambda st: tiles(st, ks_ref, vs_ref, [i], [with_table(t0_ref, i)]), st)
    for h in heads:
        osel_ref[h] = finish(st[h])

    far_bias = lambda h, sc: sc + cbrow[h]
    st = lax.cond(
        i >= 1,
        lambda st: tiles(st, kw_ref, vw_ref, [i, i - 1], [with_table(t0_ref), with_table(t1_ref)]),
        lambda st: tiles(st, kw_ref, vw_ref, [i], [with_table(t0_ref)]), init)
    st = lax.cond(
        i >= 3,
        lambda st: tiles(st, kw_ref, vw_ref, [i - 2, i - 3], [far_bias, far_bias]),
        lambda st: when(i >= 2, lambda st: tiles(st, kw_ref, vw_ref, [i - 2], [far_bias]), st), st)
    st = when(i >= 4, lambda st: tiles(st, kw_ref, vw_ref, [i - 4], [with_table(t4_ref)]), st)

    sig = jax.nn.sigmoid(ngt_ref[0])
    for h in heads:
        o_cmp, o_sel, o_win = ocmp_ref[h], osel_ref[h], finish(st[h])
        for g in range(GROUP):
            head = GROUP * h + g
            ls = slice(g * Q_BLOCK, (g + 1) * Q_BLOCK)
            o = (sig[head:head + 1, :] * o_cmp[:, ls] + sig[N_HEADS + head:N_HEADS + head + 1, :] * o_sel[:, ls]
                 + sig[2 * N_HEADS + head:2 * N_HEADS + head + 1, :] * o_win[:, ls])
            o_ref[0, head * HEAD_DIM:(head + 1) * HEAD_DIM, :] = o


def _attn_prompt(qt, ngt, kvt, kcv, tabs, c31, c2st, bsz, nq):
    nch = kcv.shape[2]
    nb = c2st.shape[0]
    width = GROUP * Q_BLOCK
    kvspec = lambda k: pl.BlockSpec((nq, D_KV, LANE), lambda b, i: (b, k, 0))
    tab = lambda k: pl.BlockSpec((1, N_KV, Q_BLOCK, width), lambda b, i: (nq + k, 0, 0, 0))
    return pl.pallas_call(
        functools.partial(_attn_prompt_kernel, nc=nch - 1, nb=nb),
        grid=(bsz, nq),
        in_specs=[pl.BlockSpec(memory_space=pltpu.SMEM),
                  pl.BlockSpec((1, D_ATT, LANE), lambda b, i: (b * nq + i, 0, 0)),
                  pl.BlockSpec((1, NG_ROWS, LANE), lambda b, i: (b * nq + i, 0, 0)),
                  pl.BlockSpec((1, 2 * D_KV, nch), lambda b, i: (b, 0, 0)),
                  kvspec(2), kvspec(3), kvspec(4), kvspec(5),
                  pl.BlockSpec((1, N_KV, Q_BLOCK, width), lambda b, i: (i, 0, 0, 0)),
                  tab(0), tab(1), tab(2),
                  pl.BlockSpec(c2st.shape, lambda b, i: (0, 0))],
        out_specs=pl.BlockSpec((1, D_ATT, LANE), lambda b, i: (b * nq + i, 0, 0)),
        out_shape=jax.ShapeDtypeStruct((bsz * nq, D_ATT, LANE), F32),
        scratch_shapes=[pltpu.VMEM((N_KV, HEAD_DIM, width), BF16),
                        pltpu.VMEM((N_KV, nb, width), F32),
                        pltpu.VMEM((N_KV, HEAD_DIM, width), F32),
                        pltpu.VMEM((N_KV, HEAD_DIM, width), F32)],
        compiler_params=_params(("arbitrary", "arbitrary")),
        name="attn_prompt",
    )(c31, qt, ngt, kcv, kvt, kvt, kvt, kvt, tabs, tabs, tabs, tabs, c2st)


def _softmax_lanes(s, mask):
    s = jnp.where(mask, s, NEG)
    p = jnp.where(mask, jnp.exp(s - jnp.max(s, axis=-1, keepdims=True)), 0.0)
    return p / jnp.maximum(jnp.sum(p, axis=-1, keepdims=True), 1e-30)


def _attn_sample_kernel(pt_ref, *refs, npg, s, p, wb, nb, aliased):
    del pt_ref
    pg_refs = refs[:npg]
    (qbd_ref, ngt_ref, new_ref, cw_ref, bdc_ref, pec_ref, bc_ref, bs_ref, bw_ref,
     c2s_ref, gsum_ref, e_ref) = refs[npg:npg + 12]
    o_ref, win_ref, xs_ref = refs[npg + 12 + int(aliased):]
    nl = N_HEADS * s
    nch = npg * CPP
    n_sel = min(N_SEL, nb)
    qb = (qbd_ref[0] * HEAD_DIM ** -0.5).astype(BF16)

    def row_t(shape):
        return lax.broadcasted_iota(jnp.int32, shape, 0) % s

    def lanes_of(shape):
        return lax.broadcasted_iota(jnp.int32, shape, 1)

    def pad_rows(x):
        return jnp.concatenate([x, jnp.zeros((LANE - x.shape[0], x.shape[1]), x.dtype)], axis=0).astype(BF16)

    kct, vct = _compress_core(lambda sec, pg: pg_refs[pg][0, 0, sec * D_KV:(sec + 1) * D_KV, :],
                              npg, xs_ref, bdc_ref, pec_ref)

    sc = _dot(qb, kct.astype(BF16)) + bc_ref[...]
    c_l = lanes_of((nl, nch))
    dc = p + row_t((nl, nch)) - CMP_STRIDE * c_l - (CMP_BLOCK - 1)
    pc = _softmax_lanes(sc, (dc >= 0) & (c_l < nch - 1))
    o_c = _dot_nt(pc.astype(BF16), vct.astype(BF16))

    imp = _dot_hi(gsum_ref[...], _dot_hi(pc, c2s_ref[...]))
    j_l = lanes_of((nl, LANE))
    qpos = p + row_t((nl, LANE))
    cur = qpos // SEL_BLOCK
    valid = (j_l * SEL_BLOCK <= qpos) & (j_l < nb)
    forced = (j_l == 0) | (j_l == cur) | (j_l == cur - 1)
    score = jnp.where(valid, jnp.where(forced, FORCED, imp), -jnp.inf)
    rank = jnp.zeros((nl, LANE), jnp.int32)
    for jp in range(nb):
        cj = score[:, jp:jp + 1]
        beats = (cj > score) | ((cj == score) & (j_l > jp))
        rank = rank + beats.astype(jnp.int32)
    sel = jnp.where(rank < n_sel, 1.0, 0.0).astype(BF16)
    maskf = _dot(sel, e_ref[...])

    ss = jnp.concatenate([_dot(qb, r[0, 0, 2 * D_KV:3 * D_KV, :].astype(BF16)) for r in pg_refs]
                         + [_dot_nt(qb, pad_rows(new_ref[0, 0]))], axis=1) + bs_ref[...]
    lk = p + LANE
    ds_ = p + row_t((nl, lk)) - lanes_of((nl, lk))
    ps = _softmax_lanes(ss, (maskf > 0.5) & (ds_ >= 0)).astype(BF16)
    o_s = _dot(ps[:, p:], pad_rows(new_ref[0, 1]))
    for g, r in enumerate(pg_refs):
        o_s = o_s + _dot_nt(ps[:, g * PAGE:(g + 1) * PAGE], r[0, 0, 3 * D_KV:4 * D_KV, :].astype(BF16))

    sw = jnp.concatenate([_dot(qb, cw_ref[0, 0, 0:D_KV, :].astype(BF16)),
                          _dot_nt(qb, pad_rows(new_ref[0, 2]))], axis=1) + bw_ref[...]
    lw = wb + LANE
    w_l = lanes_of((nl, lw))
    dw = wb + row_t((nl, lw)) - w_l
    pw = _softmax_lanes(sw, (dw >= 0) & (dw < WINDOW) & (p - wb + w_l >= 0)).astype(BF16)
    o_w = _dot_nt(pw[:, :wb], cw_ref[0, 0, D_KV:2 * D_KV, :].astype(BF16)) + _dot(pw[:, wb:], pad_rows(new_ref[0, 3]))

    g = jax.nn.sigmoid(ngt_ref[0])
    o_ref[0] = g[:, 0:1] * o_c + g[:, 1:2] * o_s + g[:, 2:3] * o_w

    def new_t(x):
        xt = jnp.concatenate([x, jnp.zeros((LANE - x.shape[0], x.shape[1]), x.dtype)], axis=0).T
        return pltpu.roll(xt, LANE - s, 1)

    rolled = pltpu.roll(cw_ref[0, 0], wb - s, 1)
    newt = jnp.concatenate([new_t(new_ref[0, 2]), new_t(new_ref[0, 3])], axis=0)
    lane = lax.broadcasted_iota(jnp.int32, newt.shape, 1)
    last = jnp.where(lane >= LANE - s, newt, rolled[:, wb - LANE:])
    win_ref[0, 0] = jnp.concatenate([rolled[:, :wb - LANE], last], axis=1)


def _attn_sample(cache_t, pt_flat, layer, qbd, ngt, newkv, cwin_t, bdc, pec, bc, bs, bw, c2s, gsum, emat, win_buf,
                 npg, s, p, wb, nb):
    dbs = qbd.shape[0]
    depth = cwin_t.shape[1]
    nl = N_HEADS * s
    nch = npg * CPP
    aliased = win_buf is not None
    pgs = [pl.BlockSpec((1, 1, 4 * D_KV, PAGE), (lambda b, pt, g=g: (pt[b * npg + g], layer, 0, 0)))
           for g in range(npg)]
    const = lambda a: pl.BlockSpec(a.shape, lambda b, pt: (0,) * a.ndim)
    win_spec = pl.BlockSpec((1, 1, 2 * D_KV, wb), lambda b, pt: (b, layer, 0, 0))
    operands = [pt_flat, *([cache_t] * npg), qbd, ngt, newkv, cwin_t, bdc, pec, bc, bs, bw, c2s, gsum, emat]
    in_specs = pgs + [
        pl.BlockSpec((1, nl, D_KV), lambda b, pt: (b, 0, 0)),
        pl.BlockSpec((1, nl, LANE), lambda b, pt: (b, 0, 0)),
        pl.BlockSpec((1,) + newkv.shape[1:], lambda b, pt: (b, 0, 0, 0)),
        win_spec,
        const(bdc), const(pec), const(bc), const(bs), const(bw), const(c2s), const(gsum), const(emat)]
    aliases = {}
    if aliased:
        aliases = {len(operands): 1}
        operands.append(win_buf)
        in_specs.append(pl.BlockSpec(memory_space=pl.ANY))
    grid_spec = pltpu.PrefetchScalarGridSpec(
        num_scalar_prefetch=1,
        grid=(dbs,),
        in_specs=in_specs,
        out_specs=[pl.BlockSpec((1, nl, D_KV), lambda b, pt: (b, 0, 0)), win_spec],
        scratch_shapes=[pltpu.VMEM((2, nch * XS_PITCH, LANE), F32)],
    )
    return pl.pallas_call(
        functools.partial(_attn_sample_kernel, npg=npg, s=s, p=p, wb=wb, nb=nb, aliased=aliased),
        grid_spec=grid_spec,
        out_shape=[jax.ShapeDtypeStruct((dbs, nl, D_KV), F32),
                   jax.ShapeDtypeStruct((dbs, depth, 2 * D_KV, wb), F32)],
        input_output_aliases=aliases,
        compiler_params=_params(("arbitrary",)),
        name="attn_sample",
    )(*operands)


def _merge_kernel(h_ref, ga_ref, att_ref, bz_ref, ma_ref, mb_ref, wa_ref, wb_ref, wo_ref, fg_ref, o_ref, *, final):
    y_a = _dot(ga_ref[...], wa_ref[...])
    xt = jnp.concatenate([att_ref[k] * _silu(bz_ref[k]) for k in range(att_ref.shape[0])], axis=1)
    y_b = _dot_tn(xt.astype(BF16), wb_ref[...])
    mix = jax.nn.sigmoid(ma_ref[...]) * y_a + jax.nn.sigmoid(mb_ref[...]) * y_b
    out = h_ref[...] + _dot(mix.astype(BF16), wo_ref[...])
    if final:
        ms = jnp.mean(out * out, axis=-1, keepdims=True)
        out = out * lax.rsqrt(ms + EPS) * fg_ref[...]
    o_ref[...] = out


def _merge(h, ga, att_t, bz_t, pm, wa, wb, wo, fg, tm, final):
    n = h.shape[0]
    tl = tm // LANE
    row = pl.BlockSpec((tm, 1024), lambda i: (i, 0))
    col = lambda k: pl.BlockSpec((tm, 1024), lambda i: (i, k))
    tsp = pl.BlockSpec((tl, D_ATT, LANE), lambda i: (i, 0, 0))
    wsp = pl.BlockSpec((1024, 1024), lambda i: (0, 0))
    return pl.pallas_call(
        functools.partial(_merge_kernel, final=final),
        grid=(n // tm,),
        in_specs=[row, row, tsp, tsp, col(COL_MA), col(COL_MB),
                  wsp, wsp, wsp, pl.BlockSpec((1, 1024), lambda i: (0, 0))],
        out_specs=row,
        out_shape=jax.ShapeDtypeStruct((n, D_MODEL), F32),
        compiler_params=_params(("arbitrary",)),
        name="merge",
    )(h, ga, att_t, bz_t, pm, pm, wa, wb, wo, fg)


def _cmp_to_sel(nc, nb):
    c0 = np.arange(nc)[:, None] * CMP_STRIDE
    j0 = np.arange(nb)[None, :] * SEL_BLOCK
    ov = np.minimum(c0 + CMP_BLOCK, j0 + SEL_BLOCK) - np.maximum(c0, j0)
    return np.maximum(ov, 0).astype(np.float32) / CMP_BLOCK


def _largest_divisor(n, cap, mult=1):
    for d in range(min(n, cap), 0, -1):
        if n % d == 0 and d % mult == 0:
            return d
    raise ValueError("no tile size")


def _tokens_major(xt):
    return xt.transpose(0, 2, 1).reshape(xt.shape[0] * LANE, xt.shape[1])


def kernel(x_prompt, x_sample, cache_kv, cache_win, state_conv, page_table, rel_bias, norm_g, w_in, conv_w,
           w_cmp, pe_cmp, w_a_out, w_b_out, w_o, final_g):
    bsz, t, _ = x_prompt.shape
    dbs, s, _ = x_sample.shape
    depth = w_in.shape[0]
    n_pool = cache_kv.shape[0]
    npg = page_table.shape[1]
    p = npg * PAGE
    wb = cache_win.shape[2]
    nq = t // Q_BLOCK
    nch_p = t // CMP_STRIDE
    nch_s = (p + s) // CMP_STRIDE
    nb_p = -(-t // SEL_BLOCK)
    nb_s = -(-(p + s) // SEL_BLOCK)
    ns = dbs * s
    assert t % Q_BLOCK == 0 and nq % 2 == 0 and nch_p == LANE and nb_p % 8 == 0, "prompt length not supported"
    assert 2 <= s <= 8 and wb == min(WINDOW, p) and nch_s == p // CMP_STRIDE == LANE and nb_s <= LANE
    assert ns % LANE == 0 and wb == WINDOW and wb % LANE == 0, "sample shape not supported"
    assert WINDOW == 4 * Q_BLOCK and SEL_BLOCK * 2 == Q_BLOCK
    nl = N_HEADS * s
    lk, lw = p + LANE, wb + LANE

    ng_perm = np.array([_O_NG + head * 3 + br for br in range(3) for head in range(N_HEADS)])
    perm_nat = np.concatenate([np.arange(0, _O_Q), np.arange(_O_MA, _O_END)])
    perm_tr = np.concatenate([np.arange(_O_Q, _O_KC), np.arange(_O_BZ, _O_MA), np.arange(_O_KC, _O_NG), ng_perm])
    w_nat = w_in[:, :, perm_nat].astype(BF16)
    w_tr = jnp.pad(w_in.transpose(0, 2, 1)[:, perm_tr, :],
                   ((0, 0), (0, D_TR - perm_tr.size), (0, 0))).astype(BF16)
    wa_b, wb_b, wo_b = w_a_out.astype(BF16), w_b_out.astype(BF16), w_o.astype(BF16)
    eye_kv = jnp.eye(N_KV, dtype=F32)
    bd = jnp.einsum('lsjde,hk->lsjhdke', w_cmp, eye_kv).reshape(depth, 2, CMP_BLOCK, D_KV, D_KV).astype(BF16)
    bdc = jnp.concatenate([bd[:, :, :CMP_STRIDE], bd[:, :, CMP_STRIDE:]], axis=-1)
    pe_t = jnp.tile(pe_cmp, (1, 1, 1, N_KV))
    pec = _pe_const(bdc, pe_t)
    g2 = norm_g.reshape(depth, 1, D_MODEL)
    fg2 = final_g.reshape(1, D_MODEL)

    tabs = _bias_prompt(rel_bias, nq)
    c31 = rel_bias[N_BUCKETS - 1]
    rbx = jnp.repeat(rel_bias.T, s, axis=0)
    bc, bs, bw = _bias_sample(rbx, s, p, wb, nch_s, lk, lw)
    c2st_p = np.zeros((nb_p, LANE), np.float32)
    c2st_p[:, :nch_p - 1] = _cmp_to_sel(nch_p - 1, nb_p).T
    c2s_s = np.zeros((LANE, LANE), np.float32)
    c2s_s[:nch_s - 1, :nb_s] = _cmp_to_sel(nch_s - 1, nb_s)
    rl = np.arange(nl)
    gsum = ((rl[:, None] // (GROUP * s) == rl[None, :] // (GROUP * s))
            & (rl[:, None] % s == rl[None, :] % s)).astype(np.float32)
    emat_s = (np.arange(lk)[None, :] // SEL_BLOCK == np.arange(LANE)[:, None])
    c2st_p, c2s_s, gsum = jnp.asarray(c2st_p), jnp.asarray(c2s_s), jnp.asarray(gsum)
    emat_s = jnp.asarray(emat_s, dtype=BF16)

    cache_t = cache_kv.transpose(0, 1, 3, 4, 5, 2).reshape(n_pool, depth, 4 * D_KV, PAGE)
    cwin_t = cache_win.transpose(0, 1, 3, 4, 5, 2).reshape(dbs, depth, 2 * D_KV, wb)
    pt_flat = page_table.reshape(-1).astype(jnp.int32)

    tm_p = _largest_divisor(bsz * t, 1024, LANE)
    tt = _largest_divisor(t, 512, 8)
    tm_m = _largest_divisor(bsz * t, 512, LANE)

    hp = x_prompt.reshape(bsz * t, D_MODEL)
    hs = x_sample.reshape(ns, D_MODEL)
    kv_p, kv_s, conv_p, conv_s = [], [], [], []
    win_buf = None
    zs = jnp.zeros((dbs, s - 1, D_CONV), F32)
    for l in range(depth):
        final = l == depth - 1
        pm, qt, bzt, kvt, ngt = _inproj(hp, g2[l], w_nat[l], w_tr[l], tm_p)
        ga, tail = _conv_prompt(pm.reshape(bsz, t, D_NAT), conv_w[l], tt)
        kcv = _compress_prompt(kvt, bdc[l], pec[l], bsz, nq)
        att_t = _attn_prompt(qt, ngt, kvt, kcv, tabs, c31, c2st_p, bsz, nq)
        hp = _merge(hp, ga.reshape(bsz * t, D_CONV), att_t, bzt, pm, wa_b[l], wb_b[l], wo_b[l], fg2, tm_m, final)
        kv_p.append(kvt)
        conv_p.append(tail[:, 8 - (CONV_W - 1):])

        pm_s, qt_s, bzt_s, kvt_s, ngt_s = _inproj(hs, g2[l], w_nat[l], w_tr[l], ns)
        st = state_conv[:, l]
        s1 = jnp.concatenate([st[:, 1:2], zs], axis=1).reshape(ns, D_CONV)
        s2 = jnp.concatenate([st, zs[:, :s - 2]], axis=1).reshape(ns, D_CONV)
        ga_s, u_s = _conv_sample(pm_s, s1, s2, conv_w[l], s)
        q_s = _tokens_major(qt_s).reshape(dbs, s, N_KV, GROUP, HEAD_DIM)
        q_r = q_s.transpose(0, 2, 3, 1, 4).reshape(dbs, N_KV, GROUP * s, HEAD_DIM)
        qbd = jnp.einsum('bkmd,kj->bkmjd', q_r, eye_kv).reshape(dbs, nl, D_KV)
        ng_s = _tokens_major(ngt_s)[:, :3 * N_HEADS].reshape(dbs, s, 3, N_HEADS).transpose(0, 3, 1, 2)
        ng_s = jnp.pad(ng_s.reshape(dbs, nl, 3), ((0, 0), (0, 0), (0, LANE - 3)))
        kv_nat_s = _tokens_major(kvt_s).reshape(dbs, s, 6, D_KV)
        newkv = jnp.pad(kv_nat_s[:, :, 2:].transpose(0, 2, 1, 3), ((0, 0), (0, 0), (0, 8 - s), (0, 0)))
        o_s, win_buf = _attn_sample(cache_t, pt_flat, l, qbd, ng_s, newkv, cwin_t, bdc[l], pec[l], bc, bs, bw,
                                    c2s_s, gsum, emat_s, win_buf, npg, s, p, wb, nb_s)
        o6 = o_s.reshape(dbs, N_KV, GROUP, s, N_KV, HEAD_DIM)
        att_s = jnp.stack([o6[:, k, :, :, k] for k in range(N_KV)], axis=1)
        att_s = att_s.transpose(0, 3, 1, 2, 4).reshape(ns // LANE, LANE, D_ATT).transpose(0, 2, 1)
        hs = _merge(hs, ga_s, att_s, bzt_s, pm_s, wa_b[l], wb_b[l], wo_b[l], fg2, ns, final)
        kv_s.append(kv_nat_s[:, :, :4].reshape(dbs, s, 4, N_KV, HEAD_DIM))
        conv_s.append(u_s.reshape(dbs, s, D_CONV)[:, s - (CONV_W - 1):])

    y_prompt = hp.reshape(bsz, t, D_MODEL)
    y_sample = hs.reshape(dbs, s, D_MODEL)
    win_sample = win_buf.reshape(dbs, depth, 2, N_KV, HEAD_DIM, wb).transpose(0, 1, 5, 2, 3, 4)
    kv_all = jnp.stack(kv_p, axis=0).reshape(depth, bsz, nq, 6, N_KV, HEAD_DIM, LANE)
    kv_all = kv_all.transpose(1, 0, 2, 6, 3, 4, 5).reshape(bsz, depth, t, 6, N_KV, HEAD_DIM)
    kv_prompt = kv_all[:, :, :, :4]
    win_prompt = kv_all[:, :, t - min(WINDOW, t):, 4:]
    return (y_prompt, y_sample, kv_prompt, jnp.stack(kv_s, axis=1), win_prompt,
            win_sample, jnp.stack(conv_p, axis=1), jnp.stack(conv_s, axis=1))
```

```python
import functools
import math

import numpy as np
import jax
import jax.numpy as jnp
from jax import lax
from jax.experimental import pallas as pl
from jax.experimental.pallas import tpu as pltpu

F32 = jnp.float32
BF16 = jnp.bfloat16

D_MODEL = 1024
N_HEADS = 16
N_KV = 4
GROUP = N_HEADS // N_KV
HEAD_DIM = 64
D_ATT = N_HEADS * HEAD_DIM
D_KV = N_KV * HEAD_DIM
D_CONV = D_MODEL
CONV_W = 3
CMP_BLOCK = 32
CMP_STRIDE = 16
SEL_BLOCK = 64
N_SEL = 8
WINDOW = 512
Q_BLOCK = 128
FORCED = 1e4
N_BUCKETS = 32
MAX_DISTANCE = 128
EPS = 1e-6
NEG = -1e30
LOG2E = math.log2(math.e)
PAGE = 128
CPP = PAGE // CMP_STRIDE

LANE = 128
TN = 512
TN_NAT = 1024
COL_B, COL_C, COL_H, COL_ZA, COL_MA, COL_MB = range(6)
D_NAT = 6 * 1024
D_TR = 8 * TN
NG_ROWS = 64
_O_Q, _O_KC, _O_NG, _O_BZ, _O_MA, _O_END = 4096, 5120, 6656, 6704, 7728, 9776

VMEM_LIMIT = 56 * 1024 * 1024
XS_PITCH = 24

_EXACT = N_BUCKETS // 2
_THRESH = tuple(int(math.ceil(_EXACT * (MAX_DISTANCE / _EXACT) ** (k / (N_BUCKETS - _EXACT))))
                for k in range(1, N_BUCKETS - _EXACT))


def _params(sem, vmem=VMEM_LIMIT):
    return pltpu.CompilerParams(dimension_semantics=sem, vmem_limit_bytes=vmem)


def _bucket(d):
    d = jnp.maximum(d, 0)
    big = jnp.full(d.shape, _EXACT, jnp.int32)
    for thr in _THRESH:
        big = big + (d >= thr).astype(jnp.int32)
    return jnp.where(d < _EXACT, d, big)


def _dot(a, b):
    return jnp.dot(a, b, preferred_element_type=F32)


def _dot_nt(a, b):
    return lax.dot_general(a, b, (((1,), (1,)), ((), ())), preferred_element_type=F32)


def _dot_tn(a, b):
    return lax.dot_general(a, b, (((0,), (0,)), ((), ())), preferred_element_type=F32)


def _dot_hi(a, b):
    return jnp.dot(a, b, preferred_element_type=F32, precision=lax.Precision.HIGHEST)


def _silu(z):
    return z * jax.nn.sigmoid(z)


def _inproj_kernel(x_ref, g_ref, w_ref, wt_ref, main_ref, qt_ref, bzt_ref, kvt_ref, ngt_ref, xn_ref, *, n_nat):
    j = pl.program_id(1)
    ntile = qt_ref.shape[0]

    @pl.when(j == 0)
    def _():
        x = x_ref[...]
        ms = jnp.mean(x * x, axis=-1, keepdims=True)
        xn_ref[...] = (x * lax.rsqrt(ms + EPS) * g_ref[...]).astype(BF16)

    @pl.when(j < n_nat)
    def _():
        main_ref[...] = _dot(xn_ref[...], w_ref[...])

    @pl.when(j >= n_nat)
    def _():
        yt = _dot_nt(wt_ref[...], xn_ref[...])

        def put(ref, rows):
            for k in range(ntile):
                ref[k] = yt[:rows, k * LANE:(k + 1) * LANE]

        jt = j - n_nat

        @pl.when(jt < 2)
        def _():
            put(qt_ref, TN)

        @pl.when((jt >= 2) & (jt < 4))
        def _():
            put(bzt_ref, TN)

        @pl.when((jt >= 4) & (jt < 7))
        def _():
            put(kvt_ref, TN)

        @pl.when(jt == 7)
        def _():
            put(ngt_ref, NG_ROWS)


def _inproj(x, g, w_nat, w_tr, tm):
    n = x.shape[0]
    n_nat = D_NAT // TN_NAT
    nt = n // LANE
    tl = tm // LANE
    clampi = lambda v, lo, hi: jnp.minimum(jnp.maximum(v, lo), hi)
    return pl.pallas_call(
        functools.partial(_inproj_kernel, n_nat=n_nat),
        grid=(n // tm, n_nat + D_TR // TN),
        in_specs=[pl.BlockSpec((tm, D_MODEL), lambda i, j: (i, 0)),
                  pl.BlockSpec((1, D_MODEL), lambda i, j: (0, 0)),
                  pl.BlockSpec((D_MODEL, TN_NAT), lambda i, j: (0, jnp.minimum(j, n_nat - 1))),
                  pl.BlockSpec((TN, D_MODEL), lambda i, j: (jnp.maximum(j - n_nat, 0), 0))],
        out_specs=[pl.BlockSpec((tm, TN_NAT), lambda i, j: (i, jnp.minimum(j, n_nat - 1))),
                   pl.BlockSpec((tl, TN, LANE), lambda i, j: (i, clampi(j - n_nat, 0, 1), 0)),
                   pl.BlockSpec((tl, TN, LANE), lambda i, j: (i, clampi(j - n_nat - 2, 0, 1), 0)),
                   pl.BlockSpec((tl, TN, LANE), lambda i, j: (i, clampi(j - n_nat - 4, 0, 2), 0)),
                   pl.BlockSpec((tl, NG_ROWS, LANE), lambda i, j: (i, 0, 0))],
        out_shape=[jax.ShapeDtypeStruct((n, D_NAT), F32),
                   jax.ShapeDtypeStruct((nt, D_ATT, LANE), F32),
                   jax.ShapeDtypeStruct((nt, D_ATT, LANE), F32),
                   jax.ShapeDtypeStruct((nt, 6 * D_KV, LANE), F32),
                   jax.ShapeDtypeStruct((nt, NG_ROWS, LANE), F32)],
        scratch_shapes=[pltpu.VMEM((tm, D_MODEL), BF16)],
        compiler_params=_params(("arbitrary", "arbitrary")),
        name="inproj",
    )(x, g, w_nat, w_tr)


def _conv_prompt_kernel(b_ref, c_ref, h_ref, z_ref, cp_ref, hp_ref, w_ref, ga_ref, tail_ref):
    t = pl.program_id(1)
    u = c_ref[0] * h_ref[0]
    up = jnp.where(t > 0, cp_ref[0] * hp_ref[0], 0.0)
    tt = u.shape[0]
    row = lax.broadcasted_iota(jnp.int32, u.shape, 0)
    um1 = jnp.where(row == 0, up[7:8], pltpu.roll(u, 1, 0))
    um2 = jnp.where(row == 0, up[6:7], jnp.where(row == 1, up[7:8], pltpu.roll(u, 2, 0)))
    w = w_ref[...]
    conv = w[0:1] * um2 + w[1:2] * um1 + w[2:3] * u
    ga_ref[0] = (b_ref[0] * conv * _silu(z_ref[0])).astype(BF16)

    @pl.when(t == pl.num_programs(1) - 1)
    def _():
        tail_ref[0] = u[tt - 8:]


def _conv_prompt(p3, conv_w, tt):
    bsz, t, _ = p3.shape
    hb = tt // 8
    col = lambda k: (lambda b, i: (b, i, k))
    halo = lambda k: (lambda b, i: (b, jnp.maximum(i * hb - 1, 0), k))
    return pl.pallas_call(
        _conv_prompt_kernel,
        grid=(bsz, t // tt),
        in_specs=[pl.BlockSpec((1, tt, 1024), col(COL_B)),
                  pl.BlockSpec((1, tt, 1024), col(COL_C)),
                  pl.BlockSpec((1, tt, 1024), col(COL_H)),
                  pl.BlockSpec((1, tt, 1024), col(COL_ZA)),
                  pl.BlockSpec((1, 8, 1024), halo(COL_C)),
                  pl.BlockSpec((1, 8, 1024), halo(COL_H)),
                  pl.BlockSpec((CONV_W, D_CONV), lambda b, i: (0, 0))],
        out_specs=[pl.BlockSpec((1, tt, 1024), lambda b, i: (b, i, 0)),
                   pl.BlockSpec((1, 8, 1024), lambda b, i: (b, 0, 0))],
        out_shape=[jax.ShapeDtypeStruct((bsz, t, D_CONV), BF16),
                   jax.ShapeDtypeStruct((bsz, 8, D_CONV), F32)],
        compiler_params=_params(("arbitrary", "arbitrary")),
        name="conv_prompt",
    )(p3, p3, p3, p3, p3, p3, conv_w)


def _conv_sample_kernel(b_ref, c_ref, h_ref, z_ref, s1_ref, s2_ref, w_ref, ga_ref, u_ref, *, s):
    u = c_ref[...] * h_ref[...]
    t = lax.broadcasted_iota(jnp.int32, u.shape, 0) % s
    um1 = jnp.where(t >= 1, pltpu.roll(u, 1, 0), s1_ref[...])
    um2 = jnp.where(t >= 2, pltpu.roll(u, 2, 0), s2_ref[...])
    w = w_ref[...]
    conv = w[0:1] * um2 + w[1:2] * um1 + w[2:3] * u
    ga_ref[...] = (b_ref[...] * conv * _silu(z_ref[...])).astype(BF16)
    u_ref[...] = u


def _conv_sample(p2, s1, s2, conv_w, s):
    n = p2.shape[0]
    col = lambda k: pl.BlockSpec((n, 1024), lambda i: (0, k))
    full = pl.BlockSpec((n, 1024), lambda i: (0, 0))
    return pl.pallas_call(
        functools.partial(_conv_sample_kernel, s=s),
        grid=(1,),
        in_specs=[col(COL_B), col(COL_C), col(COL_H), col(COL_ZA), full, full,
                  pl.BlockSpec((CONV_W, D_CONV), lambda i: (0, 0))],
        out_specs=[full, full],
        out_shape=[jax.ShapeDtypeStruct((n, D_CONV), BF16),
                   jax.ShapeDtypeStruct((n, D_CONV), F32)],
        compiler_params=_params(("arbitrary",)),
        name="conv_sample",
    )(p2, p2, p2, p2, s1, s2, conv_w)


def _compress_core(get_page, npg, xs_ref, bdc_ref, pec_ref):
    nch = npg * CPP
    outs = []
    for sec in range(2):
        for pg in range(npg):
            x = get_page(sec, pg).T
            for half in range(2):
                for c in range(CPP):
                    r0 = (pg * CPP + c) * XS_PITCH
                    xs_ref[half, r0:r0 + CMP_STRIDE, :] = x[c * CMP_STRIDE:(c + 1) * CMP_STRIDE,
                                                            half * LANE:(half + 1) * LANE]
        acc = jnp.zeros((nch, 2 * D_KV), F32)
        for j in range(CMP_STRIDE):
            xj = jnp.concatenate([xs_ref[half, pl.ds(j, nch, stride=XS_PITCH), :] for half in range(2)], axis=1)
            acc = acc + _dot(xj.astype(BF16), bdc_ref[sec, j])
        acc = acc + pec_ref[sec, 0:1, :]
        out = acc[:, :D_KV] + pltpu.roll(acc[:, D_KV:], nch - 1, 0)
        row = lax.broadcasted_iota(jnp.int32, out.shape, 0)
        out = jnp.where(row == nch - 1, 0.0, out)
        outs.append(out.T)
    return outs


def _pe_const_kernel(bdc_ref, pe_ref, o_ref):
    for sec in range(2):
        acc = jnp.zeros((8, 2 * D_KV), F32)
        for j in range(CMP_STRIDE):
            w = bdc_ref[0, sec, j]
            lo = _dot(jnp.broadcast_to(pe_ref[0, sec, j:j + 1, :], (8, D_KV)).astype(BF16), w[:, :D_KV])
            hi = _dot(jnp.broadcast_to(pe_ref[0, sec, CMP_STRIDE + j:CMP_STRIDE + j + 1, :], (8, D_KV)).astype(BF16),
                      w[:, D_KV:])
            acc = acc + jnp.concatenate([lo, hi], axis=1)
        o_ref[0, sec] = acc


def _pe_const(bdc, pe):
    depth = bdc.shape[0]
    return pl.pallas_call(
        _pe_const_kernel,
        grid=(depth,),
        in_specs=[pl.BlockSpec((1,) + bdc.shape[1:], lambda l: (l, 0, 0, 0, 0)),
                  pl.BlockSpec((1,) + pe.shape[1:], lambda l: (l, 0, 0, 0))],
        out_specs=pl.BlockSpec((1, 2, 8, 2 * D_KV), lambda l: (l, 0, 0, 0)),
        out_shape=jax.ShapeDtypeStruct((depth, 2, 8, 2 * D_KV), F32),
        compiler_params=_params(("arbitrary",)),
        name="pe_const",
    )(bdc, pe)


def _compress_prompt_kernel(kv_ref, bdc_ref, pec_ref, o_ref, xs_ref):
    npg = kv_ref.shape[0]
    kct, vct = _compress_core(lambda sec, pg: kv_ref[pg, sec * D_KV:(sec + 1) * D_KV, :], npg, xs_ref, bdc_ref, pec_ref)
    o_ref[0, 0:D_KV, :] = kct
    o_ref[0, D_KV:2 * D_KV, :] = vct


def _compress_prompt(kvt, bdc, pec, bsz, nq):
    nch = nq * CPP
    return pl.pallas_call(
        _compress_prompt_kernel,
        grid=(bsz,),
        in_specs=[pl.BlockSpec((nq, 2 * D_KV, LANE), lambda b: (b, 0, 0)),
                  pl.BlockSpec(bdc.shape, lambda b: (0, 0, 0, 0)),
                  pl.BlockSpec(pec.shape, lambda b: (0, 0, 0))],
        out_specs=pl.BlockSpec((1, 2 * D_KV, nch), lambda b: (b, 0, 0)),
        out_shape=jax.ShapeDtypeStruct((bsz, 2 * D_KV, nch), F32),
        scratch_shapes=[pltpu.VMEM((2, nch * XS_PITCH, LANE), F32)],
        compiler_params=_params(("arbitrary",)),
        name="compress_prompt",
    )(kvt, bdc, pec)


def _bias_prompt_kernel(rb_ref, o_ref, *, nq):
    s = pl.program_id(0)
    r = lax.broadcasted_iota(jnp.int32, (Q_BLOCK, LANE), 0)
    c = lax.broadcasted_iota(jnp.int32, (Q_BLOCK, LANE), 1)
    d_cmp = s * Q_BLOCK + c - CMP_STRIDE * r - (CMP_BLOCK - 1)
    d = jnp.where(s < nq, d_cmp, jnp.where(s == nq, c - r, jnp.where(s == nq + 1, Q_BLOCK + c - r, WINDOW + c - r)))
    hidden = jnp.where(s == nq, (c < r).astype(jnp.int32), jnp.where(s == nq + 2, (r <= c).astype(jnp.int32), 0))
    visible = hidden == 0
    bk = _bucket(d)
    for head in range(N_HEADS):
        acc = jnp.zeros((Q_BLOCK, LANE), F32)
        for b in range(N_BUCKETS):
            acc = jnp.where(bk == b, rb_ref[b, head] * LOG2E, acc)
        g = head % GROUP
        o_ref[0, head // GROUP, :, g * Q_BLOCK:(g + 1) * Q_BLOCK] = jnp.where(visible, acc, NEG)


def _bias_prompt(rel_bias, nq):
    return pl.pallas_call(
        functools.partial(_bias_prompt_kernel, nq=nq),
        grid=(nq + 3,),
        in_specs=[pl.BlockSpec(memory_space=pltpu.SMEM)],
        out_specs=pl.BlockSpec((1, N_KV, Q_BLOCK, GROUP * Q_BLOCK), lambda s: (s, 0, 0, 0)),
        out_shape=jax.ShapeDtypeStruct((nq + 3, N_KV, Q_BLOCK, GROUP * Q_BLOCK), F32),
        compiler_params=_params(("arbitrary",)),
        name="bias_prompt",
    )(rel_bias)


def _bias_sample_kernel(rbx_ref, bc_ref, bs_ref, bw_ref, *, s, p, wb):
    def table(o_ref, dfn):
        shape = o_ref.shape
        t = lax.broadcasted_iota(jnp.int32, shape, 0) % s
        k = lax.broadcasted_iota(jnp.int32, shape, 1)
        bk = _bucket(dfn(t, k))
        acc = jnp.zeros(shape, F32)
        for b in range(N_BUCKETS):
            acc = jnp.where(bk == b, rbx_ref[:, b:b + 1], acc)
        o_ref[...] = acc

    table(bc_ref, lambda t, k: p + t - CMP_STRIDE * k - (CMP_BLOCK - 1))
    table(bs_ref, lambda t, k: p + t - k)
    table(bw_ref, lambda t, k: wb + t - k)


def _bias_sample(rbx, s, p, wb, nch, lk, lw):
    nl = N_HEADS * s
    return pl.pallas_call(
        functools.partial(_bias_sample_kernel, s=s, p=p, wb=wb),
        out_shape=[jax.ShapeDtypeStruct((nl, nch), F32),
                   jax.ShapeDtypeStruct((nl, lk), F32),
                   jax.ShapeDtypeStruct((nl, lw), F32)],
        compiler_params=pltpu.CompilerParams(vmem_limit_bytes=VMEM_LIMIT),
        name="bias_sample",
    )(rbx)


def _attn_prompt_kernel(c31_ref, qt_ref, ngt_ref, kcv_ref, ks_ref, vs_ref, kw_ref, vw_ref, tc_ref, t0_ref, t1_ref, t4_ref,
                        c2st_ref, o_ref, q_scr, selb_ref, ocmp_ref, osel_ref, *, nc, nb):
    i = pl.program_id(1)
    width = GROUP * Q_BLOCK
    key = lax.broadcasted_iota(jnp.int32, (Q_BLOCK, width), 0)
    lq = lax.broadcasted_iota(jnp.int32, (Q_BLOCK, width), 1) % Q_BLOCK
    dc = i * Q_BLOCK + lq - CMP_STRIDE * key - (CMP_BLOCK - 1)
    mask_c = (dc >= 0) & (key < nc)

    nbp = selb_ref.shape[1]
    jj = lax.broadcasted_iota(jnp.int32, (nbp, Q_BLOCK), 0)
    qp = i * Q_BLOCK + lax.broadcasted_iota(jnp.int32, (nbp, Q_BLOCK), 1)
    cur = qp // SEL_BLOCK
    valid = (jj * SEL_BLOCK <= qp) & (jj < nb)
    forced = (jj == 0) | (jj == cur) | (jj == cur - 1)
    n_sel = min(N_SEL, nb)
    half = SEL_BLOCK
    heads = range(N_KV)
    hsl = [slice(h * HEAD_DIM, (h + 1) * HEAD_DIM) for h in heads]

    qts, ocmps, selbs = [], [], []
    for h in heads:
        qt = jnp.concatenate(
            [qt_ref[0, (GROUP * h + g) * HEAD_DIM:(GROUP * h + g + 1) * HEAD_DIM, :] for g in range(GROUP)],
            axis=1)
        qts.append((qt * (HEAD_DIM ** -0.5 * LOG2E)).astype(BF16))

    for h in heads:
        s = _dot_tn(kcv_ref[0, hsl[h], :].astype(BF16), qts[h]) + tc_ref[0, h]
        s = jnp.where(mask_c, s, NEG)
        p = jnp.where(mask_c, jnp.exp2(s - jnp.max(s, axis=0, keepdims=True)), 0.0)
        p = p / jnp.maximum(jnp.sum(p, axis=0, keepdims=True), 1e-30)
        ocmps.append(_dot(kcv_ref[0, D_KV + h * HEAD_DIM:D_KV + (h + 1) * HEAD_DIM, :].astype(BF16), p.astype(BF16)))

        psum = p[:, 0:Q_BLOCK]
        for g in range(1, GROUP):
            psum = psum + p[:, g * Q_BLOCK:(g + 1) * Q_BLOCK]
        imp = _dot_hi(c2st_ref[...], psum)
        score = jnp.where(valid, jnp.where(forced, FORCED, imp), -jnp.inf)
        rank = jnp.zeros((nbp, Q_BLOCK), jnp.int32)
        for jp in range(nb):
            rj = score[jp:jp + 1, :]
            beats = (rj > score) | ((rj == score) & (jj > jp))
            rank = rank + beats.astype(jnp.int32)
        selbs.append(jnp.concatenate([jnp.where(rank < n_sel, 0.0, NEG)] * GROUP, axis=1))

    for h in heads:
        q_scr[h] = qts[h]
        ocmp_ref[h] = ocmps[h]
        selb_ref[h] = selbs[h]

    gl = lax.broadcasted_iota(jnp.int32, (1, width), 1) // Q_BLOCK
    cbrow = []
    for h in heads:
        r = jnp.zeros((1, width), F32)
        for g in range(GROUP):
            r = jnp.where(gl == g, c31_ref[GROUP * h + g] * LOG2E, r)
        cbrow.append(r)

    init = tuple((jnp.full((1, width), NEG, F32), jnp.zeros((1, width), F32), jnp.zeros((HEAD_DIM, width), F32))
                 for _ in heads)

    def update(st, sc, v):
        m_old, l_old, acc = st
        m_new = jnp.maximum(m_old, jnp.max(sc, axis=0, keepdims=True))
        pr = jnp.exp2(sc - m_new)
        alpha = jnp.exp2(m_old - m_new)
        return (m_new, alpha * l_old + jnp.sum(pr, axis=0, keepdims=True), alpha * acc + _dot(v, pr.astype(BF16)))

    def tiles(state, k_ref, v_ref, kts, bias_ofs):
        def cat(parts, axis):
            return parts[0] if len(parts) == 1 else jnp.concatenate(parts, axis=axis)

        scs = [cat([bias_of(h, _dot_tn(k_ref[kt, hsl[h], :].astype(BF16), q_scr[h]))
                    for kt, bias_of in zip(kts, bias_ofs)], 0) for h in heads]
        vs = [cat([v_ref[kt, hsl[h], :].astype(BF16) for kt in kts], 1) for h in heads]
        return tuple(update(state[h], scs[h], vs[h]) for h in heads)

    def with_table(t_ref, sel_kt=None):
        def f(h, sc):
            sc = sc + t_ref[0, h]
            if sel_kt is None:
                return sc
            r0 = selb_ref[h, pl.ds(2 * sel_kt, 1), :]
            r1 = selb_ref[h, pl.ds(2 * sel_kt + 1, 1), :]
            return jnp.concatenate([sc[:half] + r0, sc[half:] + r1], axis=0)
        return f

    def far_sel(kt):
        def f(h, sc):
            r0 = selb_ref[h, pl.ds(2 * kt, 1), :] + cbrow[h]
            r1 = selb_ref[h, pl.ds(2 * kt + 1, 1), :] + cbrow[h]
            return jnp.concatenate([sc[:half] + r0, sc[half:] + r1], axis=0)
        return f

    def finish(st):
        return st[2] / jnp.maximum(st[1], 1e-30)

    def when(pred, fn, state):
        return lax.cond(pred, fn, lambda st: st, state)

    nfar = jnp.maximum(i - 1, 0)
    st = lax.fori_loop(
        0, nfar // 2,
        lambda k, st: tiles(st, ks_ref, vs_ref, [2 * k, 2 * k + 1], [far_sel(2 * k), far_sel(2 * k + 1)]), init)
    st = when(nfar % 2 == 1, lambda st: tiles(st, ks_ref, vs_ref, [nfar - 1], [far_sel(nfar - 1)]), st)
    st = lax.cond(
        i >= 1,
        lambda st: tiles(st, ks_ref, vs_ref, [i - 1, i], [with_table(t1_ref, i - 1), with_table(t0_ref, i)]),
        lambda st: tiles(st, ks_ref, vs_ref, [i], [with_table(t0_ref, i)]), st)
    for h in heads:
        osel_ref[h] = finish(st[h])

    far_bias = lambda h, sc: sc + cbrow[h]
    st = lax.cond(
        i >= 1,
        lambda st: tiles(st, kw_ref, vw_ref, [i, i - 1], [with_table(t0_ref), with_table(t1_ref)]),
        lambda st: tiles(st, kw_ref, vw_ref, [i], [with_table(t0_ref)]), init)
    st = lax.cond(
        i >= 3,
        lambda st: tiles(st, kw_ref, vw_ref, [i - 2, i - 3], [far_bias, far_bias]),
        lambda st: when(i >= 2, lambda st: tiles(st, kw_ref, vw_ref, [i - 2], [far_bias]), st), st)
    st = when(i >= 4, lambda st: tiles(st, kw_ref, vw_ref, [i - 4], [with_table(t4_ref)]), st)

    sig = jax.nn.sigmoid(ngt_ref[0])
    for h in heads:
        o_cmp, o_sel, o_win = ocmp_ref[h], osel_ref[h], finish(st[h])
        for g in range(GROUP):
            head = GROUP * h + g
            ls = slice(g * Q_BLOCK, (g + 1) * Q_BLOCK)
            o = (sig[head:head + 1, :] * o_cmp[:, ls] + sig[N_HEADS + head:N_HEADS + head + 1, :] * o_sel[:, ls]
                 + sig[2 * N_HEADS + head:2 * N_HEADS + head + 1, :] * o_win[:, ls])
            o_ref[0, head * HEAD_DIM:(head + 1) * HEAD_DIM, :] = o


def _attn_prompt(qt, ngt, kvt, kcv, tabs, c31, c2st, bsz, nq):
    nch = kcv.shape[2]
    nb = c2st.shape[0]
    width = GROUP * Q_BLOCK
    kvspec = lambda k: pl.BlockSpec((nq, D_KV, LANE), lambda b, i: (b, k, 0))
    tab = lambda k: pl.BlockSpec((1, N_KV, Q_BLOCK, width), lambda b, i: (nq + k, 0, 0, 0))
    return pl.pallas_call(
        functools.partial(_attn_prompt_kernel, nc=nch - 1, nb=nb),
        grid=(bsz, nq),
        in_specs=[pl.BlockSpec(memory_space=pltpu.SMEM),
                  pl.BlockSpec((1, D_ATT, LANE), lambda b, i: (b * nq + i, 0, 0)),
                  pl.BlockSpec((1, NG_ROWS, LANE), lambda b, i: (b * nq + i, 0, 0)),
                  pl.BlockSpec((1, 2 * D_KV, nch), lambda b, i: (b, 0, 0)),
                  kvspec(2), kvspec(3), kvspec(4), kvspec(5),
                  pl.BlockSpec((1, N_KV, Q_BLOCK, width), lambda b, i: (i, 0, 0, 0)),
                  tab(0), tab(1), tab(2),
                  pl.BlockSpec(c2st.shape, lambda b, i: (0, 0))],
        out_specs=pl.BlockSpec((1, D_ATT, LANE), lambda b, i: (b * nq + i, 0, 0)),
        out_shape=jax.ShapeDtypeStruct((bsz * nq, D_ATT, LANE), F32),
        scratch_shapes=[pltpu.VMEM((N_KV, HEAD_DIM, width), BF16),
                        pltpu.VMEM((N_KV, nb, width), F32),
                        pltpu.VMEM((N_KV, HEAD_DIM, width), F32),
                        pltpu.VMEM((N_KV, HEAD_DIM, width), F32)],
        compiler_params=_params(("arbitrary", "arbitrary")),
        name="attn_prompt",
    )(c31, qt, ngt, kcv, kvt, kvt, kvt, kvt, tabs, tabs, tabs, tabs, c2st)


def _softmax_lanes(s, mask):
    s = jnp.where(mask, s, NEG)
    p = jnp.where(mask, jnp.exp(s - jnp.max(s, axis=-1, keepdims=True)), 0.0)
    return p / jnp.maximum(jnp.sum(p, axis=-1, keepdims=True), 1e-30)


def _attn_sample_kernel(pt_ref, *refs, npg, s, p, wb, nb, aliased):
    del pt_ref
    pg_refs = refs[:npg]
    (qbd_ref, ngt_ref, new_ref, cw_ref, bdc_ref, pec_ref, bc_ref, bs_ref, bw_ref,
     c2s_ref, gsum_ref, e_ref) = refs[npg:npg + 12]
    o_ref, win_ref, xs_ref = refs[npg + 12 + int(aliased):]
    nl = N_HEADS * s
    nch = npg * CPP
    n_sel = min(N_SEL, nb)
    qb = (qbd_ref[0] * HEAD_DIM ** -0.5).astype(BF16)

    def row_t(shape):
        return lax.broadcasted_iota(jnp.int32, shape, 0) % s

    def lanes_of(shape):
        return lax.broadcasted_iota(jnp.int32, shape, 1)

    def pad_rows(x):
        return jnp.concatenate([x, jnp.zeros((LANE - x.shape[0], x.shape[1]), x.dtype)], axis=0).astype(BF16)

    kct, vct = _compress_core(lambda sec, pg: pg_refs[pg][0, 0, sec * D_KV:(sec + 1) * D_KV, :],
                              npg, xs_ref, bdc_ref, pec_ref)

    sc = _dot(qb, kct.astype(BF16)) + bc_ref[...]
    c_l = lanes_of((nl, nch))
    dc = p + row_t((nl, nch)) - CMP_STRIDE * c_l - (CMP_BLOCK - 1)
    pc = _softmax_lanes(sc, (dc >= 0) & (c_l < nch - 1))
    o_c = _dot_nt(pc.astype(BF16), vct.astype(BF16))

    imp = _dot_hi(gsum_ref[...], _dot_hi(pc, c2s_ref[...]))
    j_l = lanes_of((nl, LANE))
    qpos = p + row_t((nl, LANE))
    cur = qpos // SEL_BLOCK
    valid = (j_l * SEL_BLOCK <= qpos) & (j_l < nb)
    forced = (j_l == 0) | (j_l == cur) | (j_l == cur - 1)
    score = jnp.where(valid, jnp.where(forced, FORCED, imp), -jnp.inf)
    rank = jnp.zeros((nl, LANE), jnp.int32)
    for jp in range(nb):
        cj = score[:, jp:jp + 1]
        beats = (cj > score) | ((cj == score) & (j_l > jp))
        rank = rank + beats.astype(jnp.int32)
    sel = jnp.where(rank < n_sel, 1.0, 0.0).astype(BF16)
    maskf = _dot(sel, e_ref[...])

    ss = jnp.concatenate([_dot(qb, r[0, 0, 2 * D_KV:3 * D_KV, :].astype(BF16)) for r in pg_refs]
                         + [_dot_nt(qb, pad_rows(new_ref[0, 0]))], axis=1) + bs_ref[...]
    lk = p + LANE
    ds_ = p + row_t((nl, lk)) - lanes_of((nl, lk))
    ps = _softmax_lanes(ss, (maskf > 0.5) & (ds_ >= 0)).astype(BF16)
    o_s = _dot(ps[:, p:], pad_rows(new_ref[0, 1]))
    for g, r in enumerate(pg_refs):
        o_s = o_s + _dot_nt(ps[:, g * PAGE:(g + 1) * PAGE], r[0, 0, 3 * D_KV:4 * D_KV, :].astype(BF16))

    sw = jnp.concatenate([_dot(qb, cw_ref[0, 0, 0:D_KV, :].astype(BF16)),
                          _dot_nt(qb, pad_rows(new_ref[0, 2]))], axis=1) + bw_ref[...]
    lw = wb + LANE
    w_l = lanes_of((nl, lw))
    dw = wb + row_t((nl, lw)) - w_l
    pw = _softmax_lanes(sw, (dw >= 0) & (dw < WINDOW) & (p - wb + w_l >= 0)).astype(BF16)
    o_w = _dot_nt(pw[:, :wb], cw_ref[0, 0, D_KV:2 * D_KV, :].astype(BF16)) + _dot(pw[:, wb:], pad_rows(new_ref[0, 3]))

    g = jax.nn.sigmoid(ngt_ref[0])
    o_ref[0] = g[:, 0:1] * o_c + g[:, 1:2] * o_s + g[:, 2:3] * o_w

    def new_t(x):
        xt = jnp.concatenate([x, jnp.zeros((LANE - x.shape[0], x.shape[1]), x.dtype)], axis=0).T
        return pltpu.roll(xt, LANE - s, 1)

    rolled = pltpu.roll(cw_ref[0, 0], wb - s, 1)
    newt = jnp.concatenate([new_t(new_ref[0, 2]), new_t(new_ref[0, 3])], axis=0)
    lane = lax.broadcasted_iota(jnp.int32, newt.shape, 1)
    last = jnp.where(lane >= LANE - s, newt, rolled[:, wb - LANE:])
    win_ref[0, 0] = jnp.concatenate([rolled[:, :wb - LANE], last], axis=1)


def _attn_sample(cache_t, pt_flat, layer, qbd, ngt, newkv, cwin_t, bdc, pec, bc, bs, bw, c2s, gsum, emat, win_buf,
                 npg, s, p, wb, nb):
    dbs = qbd.shape[0]
    depth = cwin_t.shape[1]
    nl = N_HEADS * s
    nch = npg * CPP
    aliased = win_buf is not None
    pgs = [pl.BlockSpec((1, 1, 4 * D_KV, PAGE), (lambda b, pt, g=g: (pt[b * npg + g], layer, 0, 0)))
           for g in range(npg)]
    const = lambda a: pl.BlockSpec(a.shape, lambda b, pt: (0,) * a.ndim)
    win_spec = pl.BlockSpec((1, 1, 2 * D_KV, wb), lambda b, pt: (b, layer, 0, 0))
    operands = [pt_flat, *([cache_t] * npg), qbd, ngt, newkv, cwin_t, bdc, pec, bc, bs, bw, c2s, gsum, emat]
    in_specs = pgs + [
        pl.BlockSpec((1, nl, D_KV), lambda b, pt: (b, 0, 0)),
        pl.BlockSpec((1, nl, LANE), lambda b, pt: (b, 0, 0)),
        pl.BlockSpec((1,) + newkv.shape[1:], lambda b, pt: (b, 0, 0, 0)),
        win_spec,
        const(bdc), const(pec), const(bc), const(bs), const(bw), const(c2s), const(gsum), const(emat)]
    aliases = {}
    if aliased:
        aliases = {len(operands): 1}
        operands.append(win_buf)
        in_specs.append(pl.BlockSpec(memory_space=pl.ANY))
    grid_spec = pltpu.PrefetchScalarGridSpec(
        num_scalar_prefetch=1,
        grid=(dbs,),
        in_specs=in_specs,
        out_specs=[pl.BlockSpec((1, nl, D_KV), lambda b, pt: (b, 0, 0)), win_spec],
        scratch_shapes=[pltpu.VMEM((2, nch * XS_PITCH, LANE), F32)],
    )
    return pl.pallas_call(
        functools.partial(_attn_sample_kernel, npg=npg, s=s, p=p, wb=wb, nb=nb, aliased=aliased),
        grid_spec=grid_spec,
        out_shape=[jax.ShapeDtypeStruct((dbs, nl, D_KV), F32),
                   jax.ShapeDtypeStruct((dbs, depth, 2 * D_KV, wb), F32)],
        input_output_aliases=aliases,
        compiler_params=_params(("arbitrary",)),
        name="attn_sample",
    )(*operands)


def _merge_kernel(h_ref, ga_ref, att_ref, bz_ref, ma_ref, mb_ref, wa_ref, wb_ref, wo_ref, fg_ref, o_ref, *, final):
    y_a = _dot(ga_ref[...], wa_ref[...])
    xt = jnp.concatenate([att_ref[k] * _silu(bz_ref[k]) for k in range(att_ref.shape[0])], axis=1)
    y_b = _dot_tn(xt.astype(BF16), wb_ref[...])
    mix = jax.nn.sigmoid(ma_ref[...]) * y_a + jax.nn.sigmoid(mb_ref[...]) * y_b
    out = h_ref[...] + _dot(mix.astype(BF16), wo_ref[...])
    if final:
        ms = jnp.mean(out * out, axis=-1, keepdims=True)
        out = out * lax.rsqrt(ms + EPS) * fg_ref[...]
    o_ref[...] = out


def _merge(h, ga, att_t, bz_t, pm, wa, wb, wo, fg, tm, final):
    n = h.shape[0]
    tl = tm // LANE
    row = pl.BlockSpec((tm, 1024), lambda i: (i, 0))
    col = lambda k: pl.BlockSpec((tm, 1024), lambda i: (i, k))
    tsp = pl.BlockSpec((tl, D_ATT, LANE), lambda i: (i, 0, 0))
    wsp = pl.BlockSpec((1024, 1024), lambda i: (0, 0))
    return pl.pallas_call(
        functools.partial(_merge_kernel, final=final),
        grid=(n // tm,),
        in_specs=[row, row, tsp, tsp, col(COL_MA), col(COL_MB),
                  wsp, wsp, wsp, pl.BlockSpec((1, 1024), lambda i: (0, 0))],
        out_specs=row,
        out_shape=jax.ShapeDtypeStruct((n, D_MODEL), F32),
        compiler_params=_params(("arbitrary",)),
        name="merge",
    )(h, ga, att_t, bz_t, pm, pm, wa, wb, wo, fg)


def _cmp_to_sel(nc, nb):
    c0 = np.arange(nc)[:, None] * CMP_STRIDE
    j0 = np.arange(nb)[None, :] * SEL_BLOCK
    ov = np.minimum(c0 + CMP_BLOCK, j0 + SEL_BLOCK) - np.maximum(c0, j0)
    return np.maximum(ov, 0).astype(np.float32) / CMP_BLOCK


def _largest_divisor(n, cap, mult=1):
    for d in range(min(n, cap), 0, -1):
        if n % d == 0 and d % mult == 0:
            return d
    raise ValueError("no tile size")


def _tokens_major(xt):
    return xt.transpose(0, 2, 1).reshape(xt.shape[0] * LANE, xt.shape[1])


def kernel(x_prompt, x_sample, cache_kv, cache_win, state_conv, page_table, rel_bias, norm_g, w_in, conv_w,
           w_cmp, pe_cmp, w_a_out, w_b_out, w_o, final_g):
    bsz, t, _ = x_prompt.shape
    dbs, s, _ = x_sample.shape
    depth = w_in.shape[0]
    n_pool = cache_kv.shape[0]
    npg = page_table.shape[1]
    p = npg * PAGE
    wb = cache_win.shape[2]
    nq = t // Q_BLOCK
    nch_p = t // CMP_STRIDE
    nch_s = (p + s) // CMP_STRIDE
    nb_p = -(-t // SEL_BLOCK)
    nb_s = -(-(p + s) // SEL_BLOCK)
    ns = dbs * s
    assert t % Q_BLOCK == 0 and nq % 2 == 0 and nch_p == LANE and nb_p % 8 == 0, "prompt length not supported"
    assert 2 <= s <= 8 and wb == min(WINDOW, p) and nch_s == p // CMP_STRIDE == LANE and nb_s <= LANE
    assert ns % LANE == 0 and wb == WINDOW and wb % LANE == 0, "sample shape not supported"
    assert WINDOW == 4 * Q_BLOCK and SEL_BLOCK * 2 == Q_BLOCK
    nl = N_HEADS * s
    lk, lw = p + LANE, wb + LANE

    ng_perm = np.array([_O_NG + head * 3 + br for br in range(3) for head in range(N_HEADS)])
    perm_nat = np.concatenate([np.arange(0, _O_Q), np.arange(_O_MA, _O_END)])
    perm_tr = np.concatenate([np.arange(_O_Q, _O_KC), np.arange(_O_BZ, _O_MA), np.arange(_O_KC, _O_NG), ng_perm])
    w_nat = w_in[:, :, perm_nat].astype(BF16)
    w_tr = jnp.pad(w_in.transpose(0, 2, 1)[:, perm_tr, :],
                   ((0, 0), (0, D_TR - perm_tr.size), (0, 0))).astype(BF16)
    wa_b, wb_b, wo_b = w_a_out.astype(BF16), w_b_out.astype(BF16), w_o.astype(BF16)
    eye_kv = jnp.eye(N_KV, dtype=F32)
    bd = jnp.einsum('lsjde,hk->lsjhdke', w_cmp, eye_kv).reshape(depth, 2, CMP_BLOCK, D_KV, D_KV).astype(BF16)
    bdc = jnp.concatenate([bd[:, :, :CMP_STRIDE], bd[:, :, CMP_STRIDE:]], axis=-1)
    pe_t = jnp.tile(pe_cmp, (1, 1, 1, N_KV))
    pec = _pe_const(bdc, pe_t)
    g2 = norm_g.reshape(depth, 1, D_MODEL)
    fg2 = final_g.reshape(1, D_MODEL)

    tabs = _bias_prompt(rel_bias, nq)
    c31 = rel_bias[N_BUCKETS - 1]
    rbx = jnp.repeat(rel_bias.T, s, axis=0)
    bc, bs, bw = _bias_sample(rbx, s, p, wb, nch_s, lk, lw)
    c2st_p = np.zeros((nb_p, LANE), np.float32)
    c2st_p[:, :nch_p - 1] = _cmp_to_sel(nch_p - 1, nb_p).T
    c2s_s = np.zeros((LANE, LANE), np.float32)
    c2s_s[:nch_s - 1, :nb_s] = _cmp_to_sel(nch_s - 1, nb_s)
    rl = np.arange(nl)
    gsum = ((rl[:, None] // (GROUP * s) == rl[None, :] // (GROUP * s))
            & (rl[:, None] % s == rl[None, :] % s)).astype(np.float32)
    emat_s = (np.arange(lk)[None, :] // SEL_BLOCK == np.arange(LANE)[:, None])
    c2st_p, c2s_s, gsum = jnp.asarray(c2st_p), jnp.asarray(c2s_s), jnp.asarray(gsum)
    emat_s = jnp.asarray(emat_s, dtype=BF16)

    cache_t = cache_kv.transpose(0, 1, 3, 4, 5, 2).reshape(n_pool, depth, 4 * D_KV, PAGE)
    cwin_t = cache_win.transpose(0, 1, 3, 4, 5, 2).reshape(dbs, depth, 2 * D_KV, wb)
    pt_flat = page_table.reshape(-1).astype(jnp.int32)

    tm_p = _largest_divisor(bsz * t, 1024, LANE)
    tt = _largest_divisor(t, 512, 8)
    tm_m = _largest_divisor(bsz * t, 512, LANE)

    hp = x_prompt.reshape(bsz * t, D_MODEL)
    hs = x_sample.reshape(ns, D_MODEL)
    kv_p, kv_s, conv_p, conv_s = [], [], [], []
    win_buf = None
    zs = jnp.zeros((dbs, s - 1, D_CONV), F32)
    for l in range(depth):
        final = l == depth - 1
        pm, qt, bzt, kvt, ngt = _inproj(hp, g2[l], w_nat[l], w_tr[l], tm_p)
        ga, tail = _conv_prompt(pm.reshape(bsz, t, D_NAT), conv_w[l], tt)
        kcv = _compress_prompt(kvt, bdc[l], pec[l], bsz, nq)
        att_t = _attn_prompt(qt, ngt, kvt, kcv, tabs, c31, c2st_p, bsz, nq)
        hp = _merge(hp, ga.reshape(bsz * t, D_CONV), att_t, bzt, pm, wa_b[l], wb_b[l], wo_b[l], fg2, tm_m, final)
        kv_p.append(kvt)
        conv_p.append(tail[:, 8 - (CONV_W - 1):])

        pm_s, qt_s, bzt_s, kvt_s, ngt_s = _inproj(hs, g2[l], w_nat[l], w_tr[l], ns)
        st = state_conv[:, l]
        s1 = jnp.concatenate([st[:, 1:2], zs], axis=1).reshape(ns, D_CONV)
        s2 = jnp.concatenate([st, zs[:, :s - 2]], axis=1).reshape(ns, D_CONV)
        ga_s, u_s = _conv_sample(pm_s, s1, s2, conv_w[l], s)
        q_s = _tokens_major(qt_s).reshape(dbs, s, N_KV, GROUP, HEAD_DIM)
        q_r = q_s.transpose(0, 2, 3, 1, 4).reshape(dbs, N_KV, GROUP * s, HEAD_DIM)
        qbd = jnp.einsum('bkmd,kj->bkmjd', q_r, eye_kv).reshape(dbs, nl, D_KV)
        ng_s = _tokens_major(ngt_s)[:, :3 * N_HEADS].reshape(dbs, s, 3, N_HEADS).transpose(0, 3, 1, 2)
        ng_s = jnp.pad(ng_s.reshape(dbs, nl, 3), ((0, 0), (0, 0), (0, LANE - 3)))
        kv_nat_s = _tokens_major(kvt_s).reshape(dbs, s, 6, D_KV)
        newkv = jnp.pad(kv_nat_s[:, :, 2:].transpose(0, 2, 1, 3), ((0, 0), (0, 0), (0, 8 - s), (0, 0)))
        o_s, win_buf = _attn_sample(cache_t, pt_flat, l, qbd, ng_s, newkv, cwin_t, bdc[l], pec[l], bc, bs, bw,
                                    c2s_s, gsum, emat_s, win_buf, npg, s, p, wb, nb_s)
        o6 = o_s.reshape(dbs, N_KV, GROUP, s, N_KV, HEAD_DIM)
        att_s = jnp.stack([o6[:, k, :, :, k] for k in range(N_KV)], axis=1)
        att_s = att_s.transpose(0, 3, 1, 2, 4).reshape(ns // LANE, LANE, D_ATT).transpose(0, 2, 1)
        hs = _merge(hs, ga_s, att_s, bzt_s, pm_s, wa_b[l], wb_b[l], wo_b[l], fg2, ns, final)
        kv_s.append(kv_nat_s[:, :, :4].reshape(dbs, s, 4, N_KV, HEAD_DIM))
        conv_s.append(u_s.reshape(dbs, s, D_CONV)[:, s - (CONV_W - 1):])

    y_prompt = hp.reshape(bsz, t, D_MODEL)
    y_sample = hs.reshape(dbs, s, D_MODEL)
    win_sample = win_buf.reshape(dbs, depth, 2, N_KV, HEAD_DIM, wb).transpose(0, 1, 5, 2, 3, 4)
    def to_rows(parts, ntile, nsec):
        x = jnp.stack(parts, axis=0).reshape(depth, bsz, ntile, nsec, N_KV, HEAD_DIM, LANE)
        return x.transpose(1, 0, 2, 6, 3, 4, 5).reshape(bsz, depth, ntile * LANE, nsec, N_KV, HEAD_DIM)

    nwt = min(WINDOW, t) // LANE
    kv_prompt = to_rows([k[:, :4 * D_KV] for k in kv_p], nq, 4)
    win_prompt = to_rows([k.reshape(bsz, nq, 6 * D_KV, LANE)[:, nq - nwt:, 4 * D_KV:] for k in kv_p], nwt, 2)
    return (y_prompt, y_sample, kv_prompt, jnp.stack(kv_s, axis=1), win_prompt,
            win_sample, jnp.stack(conv_p, axis=1), jnp.stack(conv_s, axis=1))
```
